```python
import math
import jax
import jax.numpy as jnp
from jax import lax
import numpy as np

D_MODEL = 1024
BATCH = 8
SEQ = 2048
DEPTH = 4
DEC_BATCH = 128
DEC_SEQ = 1
PAST_LEN = 2048
PAGE_SIZE = 128

EPS = 1e-6
D_CONV = D_MODEL
CONV_WIDTH = 3
D_SSM = D_MODEL
SSM_GROUP = 16
SSM_GROUPS = D_SSM // SSM_GROUP
SSM_STATE = 64
HEAD_DIM = 128
N_HEADS = D_MODEL // HEAD_DIM
N_KV_HEADS = N_HEADS // 2
KV_GROUP = N_HEADS // N_KV_HEADS
Q_DIM = N_HEADS * HEAD_DIM
KV_DIM = N_KV_HEADS * HEAD_DIM
MOBA_BLOCK = 256
MOBA_TOPK = 3
Q_CHUNK = 16
ATTN_SCALE = HEAD_DIM ** -0.5
NEG = -1e30
D_FF = -(-8 * D_MODEL // (3 * 256)) * 256
IN_WIDTHS = (D_CONV, D_CONV, D_CONV, D_SSM, Q_DIM, KV_DIM, KV_DIM, 3 * D_MODEL)
IN_SPLITS = tuple(int(s) for s in np.cumsum(IN_WIDTHS)[:-1])
D_IN = int(sum(IN_WIDTHS))

kernel_name = 'hybrid_conv_s5_moba_decoder_step'


def rms_norm(x, g):
    xf = x.astype(jnp.float32)
    y = xf * lax.rsqrt(jnp.mean(xf * xf, axis=-1, keepdims=True) + EPS)
    return (y * g.astype(jnp.float32)).astype(x.dtype)


def causal_short_conv(z, prev, w):
    t = z.shape[1]
    full = jnp.concatenate([prev.astype(z.dtype), z], axis=1)
    out = sum(full[:, i:i + t] * w[i] for i in range(CONV_WIDTH))
    return out, full[:, t:]


def s5_discretize(lam_re, lam_im, log_step, b_re, b_im):
    lam = lax.complex(lam_re.astype(jnp.float32), lam_im.astype(jnp.float32))
    step = jnp.exp(log_step.astype(jnp.float32))[:, None]
    lam_bar = jnp.exp(lam * step)
    b = lax.complex(b_re.astype(jnp.float32), b_im.astype(jnp.float32))
    b_bar = ((lam_bar - 1.0) / lam)[..., None] * b
    return lam_bar, b_bar


def s5_scan(u, s0, lam_bar, b_bar, c_re, c_im, d):
    bn, t, _ = u.shape
    ug = u.astype(jnp.float32).reshape(bn, t, SSM_GROUPS, SSM_GROUP)
    bu = jnp.einsum('btgi,gpi->btgp', ug.astype(jnp.complex64), b_bar)
    bu = bu.at[:, 0].add(lam_bar * s0)
    a = jnp.broadcast_to(lam_bar, bu.shape)

    def combine(e1, e2):
        a1, b1 = e1
        a2, b2 = e2
        return a1 * a2, a2 * b1 + b2

    _, s = lax.associative_scan(combine, (a, bu), axis=1)
    c = lax.complex(c_re.astype(jnp.float32), c_im.astype(jnp.float32))
    y = jnp.einsum('btgp,gip->btgi', s, c).real + ug * d.astype(jnp.float32).reshape(SSM_GROUPS, SSM_GROUP)
    return y.reshape(bn, t, D_SSM).astype(u.dtype), s[:, -1]


def moba_attention(q, k, v):
    bn, tq = q.shape[0], q.shape[1]
    seq_len = k.shape[1]
    q_pos0 = seq_len - tq
    n_blk = -(-seq_len // MOBA_BLOCK)
    pad = n_blk * MOBA_BLOCK - seq_len
    blk_shape = (bn, n_blk, MOBA_BLOCK, N_KV_HEADS, HEAD_DIM)
    k_blk = jnp.pad(k, ((0, 0), (0, pad), (0, 0), (0, 0))).reshape(blk_shape)
    v_blk = jnp.pad(v, ((0, 0), (0, pad), (0, 0), (0, 0))).reshape(blk_shape)
    k_mean = jnp.mean(k_blk.astype(jnp.float32), axis=2)
    q_pos = q_pos0 + jnp.arange(tq)
    q_blk = q_pos // MOBA_BLOCK
    qf = q.astype(jnp.float32)
    gate = jnp.einsum('btkgd,bnkd->btkgn', qf.reshape(bn, tq, N_KV_HEADS, KV_GROUP, HEAD_DIM),
                      k_mean).reshape(bn, tq, N_HEADS, n_blk)
    past = jnp.broadcast_to((jnp.arange(n_blk)[None, :] < q_blk[:, None])[None, :, None, :], gate.shape)
    gate = jnp.where(past, gate, NEG)
    n_sel = min(MOBA_TOPK, n_blk)
    _, sel = lax.top_k(gate, n_sel)
    sel_ok = jnp.take_along_axis(past, sel, axis=-1)

    k_bh = jnp.moveaxis(k_blk, 3, 1)
    v_bh = jnp.moveaxis(v_blk, 3, 1)
    b_ix = jnp.arange(bn)[:, None, None, None]
    h_ix = (jnp.arange(N_HEADS) // KV_GROUP)[None, None, :, None]
    offs = jnp.arange(MOBA_BLOCK)
    chunk = Q_CHUNK if tq % Q_CHUNK == 0 else tq
    n_chunks = tq // chunk

    def to_chunks(a):
        return jnp.moveaxis(a.reshape((bn, n_chunks, chunk) + a.shape[2:]), 1, 0)

    def attend(args):
        qc, sel_c, ok_c, pos_c = args
        qs = qc.astype(jnp.float32) * ATTN_SCALE
        k_sel = k_bh[b_ix, h_ix, sel_c]
        v_sel = v_bh[b_ix, h_ix, sel_c]
        s_sel = jnp.einsum('bchd,bchsjd->bchsj', qs, k_sel.astype(jnp.float32))
        s_sel = jnp.where(ok_c[..., None], s_sel, NEG).reshape(bn, chunk, N_HEADS, n_sel * MOBA_BLOCK)
        own = pos_c // MOBA_BLOCK
        k_own = k_blk[:, own]
        v_own = v_blk[:, own]
        s_own = jnp.einsum('bckgd,bcjkd->bckgj', qs.reshape(bn, chunk, N_KV_HEADS, KV_GROUP, HEAD_DIM),
                           k_own.astype(jnp.float32)).reshape(bn, chunk, N_HEADS, MOBA_BLOCK)
        causal = (own[:, None] * MOBA_BLOCK + offs[None, :]) <= pos_c[:, None]
        s_own = jnp.where(causal[None, :, None, :], s_own, NEG)
        p = jax.nn.softmax(jnp.concatenate([s_sel, s_own], axis=-1), axis=-1)
        n_past = n_sel * MOBA_BLOCK
        p_sel = p[..., :n_past].reshape(bn, chunk, N_HEADS, n_sel, MOBA_BLOCK)
        p_own = p[..., n_past:].reshape(bn, chunk, N_KV_HEADS, KV_GROUP, MOBA_BLOCK)
        o = (jnp.einsum('bchsj,bchsjd->bchd', p_sel, v_sel.astype(jnp.float32))
             + jnp.einsum('bckgj,bcjkd->bckgd', p_own, v_own.astype(jnp.float32)).reshape(bn, chunk, N_HEADS, HEAD_DIM))
        return o.astype(qc.dtype)

    out = lax.map(attend, (to_chunks(q), to_chunks(sel), to_chunks(sel_ok), q_pos.reshape(n_chunks, chunk)))
    return jnp.moveaxis(out, 0, 1).reshape(bn, tq, Q_DIM)


def trunk(x, c, p, conv_state=None, ssm_re=None, ssm_im=None, cache_k=None, cache_v=None, page_table=None):
    bn, t, _ = x.shape
    new_k, new_v, new_conv, new_re, new_im = [], [], [], [], []
    c_act = jax.nn.silu(c)
    for l in range(DEPTH):
        mod = (c_act @ p['w_ada'][l] + p['b_ada'][l]).reshape(bn, 6, 1, D_MODEL)
        shift_m, scale_m, gate_m, shift_f, scale_f, gate_f = (mod[:, i] for i in range(6))
        h = rms_norm(x, p['g_mix'][l]) * (1 + scale_m) + shift_m
        proj = h @ p['w_in'][l]
        cb, cc, cx, u, q, k, v, gl = jnp.split(proj, IN_SPLITS, axis=-1)
        if conv_state is None:
            prev = jnp.zeros((bn, CONV_WIDTH - 1, D_CONV), x.dtype)
        else:
            prev = conv_state[l]
        zc, conv_last = causal_short_conv(cc * cx, prev, p['conv_w'][l])
        out_a = (cb * zc) @ p['w_conv_out'][l]
        if ssm_re is None:
            s0 = jnp.zeros((bn, SSM_GROUPS, SSM_STATE), jnp.complex64)
        else:
            s0 = lax.complex(ssm_re[l].astype(jnp.float32), ssm_im[l].astype(jnp.float32))
        lam_bar, b_bar = s5_discretize(p['ssm_lam_re'][l], p['ssm_lam_im'][l], p['ssm_log_step'][l],
                                       p['ssm_b_re'][l], p['ssm_b_im'][l])
        y_ssm, s_last = s5_scan(u, s0, lam_bar, b_bar, p['ssm_c_re'][l], p['ssm_c_im'][l], p['ssm_d'][l])
        g_ssm = jax.nn.gelu(y_ssm)
        out_b = (g_ssm @ p['w_glu_a'][l]) * jax.nn.sigmoid(g_ssm @ p['w_glu_b'][l])
        q = q.reshape(bn, t, N_HEADS, HEAD_DIM)
        k = k.reshape(bn, t, N_KV_HEADS, HEAD_DIM)
        v = v.reshape(bn, t, N_KV_HEADS, HEAD_DIM)
        if cache_k is None:
            k_all, v_all = k, v
        else:
            k_all = jnp.concatenate([cache_k[l, page_table].reshape(bn, -1, N_KV_HEADS, HEAD_DIM).astype(k.dtype), k], axis=1)
            v_all = jnp.concatenate([cache_v[l, page_table].reshape(bn, -1, N_KV_HEADS, HEAD_DIM).astype(v.dtype), v], axis=1)
        out_c = moba_attention(q, k_all, v_all)
        g_a, g_b, g_c = jnp.split(jax.nn.sigmoid(gl), 3, axis=-1)
        x = x + gate_m * ((g_a * out_a + g_b * out_b + g_c * out_c) @ p['w_o'][l])
        h = rms_norm(x, p['g_ffn'][l]) * (1 + scale_f) + shift_f
        x = x + gate_f * ((jax.nn.silu(h @ p['w_ff1'][l]) * (h @ p['w_ff3'][l])) @ p['w_ff2'][l])
        new_k.append(k)
        new_v.append(v)
        new_conv.append(conv_last)
        new_re.append(s_last.real.astype(x.dtype))
        new_im.append(s_last.imag.astype(x.dtype))
    y = rms_norm(x, p['g_final'])
    return y, jnp.stack(new_k), jnp.stack(new_v), jnp.stack(new_conv), jnp.stack(new_re), jnp.stack(new_im)


def setup_inputs(seed: int = 0) -> dict:
    key = jax.random.key(seed)
    keys = iter(jax.random.split(key, 48))
    f32 = jnp.float32

    def normal(shape, scale=1.0):
        return scale * jax.random.normal(next(keys), shape, f32)

    n_pages = PAST_LEN // PAGE_SIZE
    n_used = DEC_BATCH * n_pages
    n_pool = n_used + max(1, n_used // 4)
    page_table = jax.random.permutation(next(keys), n_pool)[:n_used].reshape(DEC_BATCH, n_pages).astype(jnp.int32)
    lam_im0 = jnp.pi * jnp.arange(SSM_STATE, dtype=f32)
    return {
        'x_prompt': normal((BATCH, SEQ, D_MODEL)),
        'x_sample': normal((DEC_BATCH, DEC_SEQ, D_MODEL)),
        'cache_k': normal((DEPTH, n_pool, PAGE_SIZE, N_KV_HEADS, HEAD_DIM)),
        'cache_v': normal((DEPTH, n_pool, PAGE_SIZE, N_KV_HEADS, HEAD_DIM)),
        'state_conv': normal((DEPTH, DEC_BATCH, CONV_WIDTH - 1, D_CONV)),
        'state_ssm_re': normal((DEPTH, DEC_BATCH, SSM_GROUPS, SSM_STATE), 0.3),
        'state_ssm_im': normal((DEPTH, DEC_BATCH, SSM_GROUPS, SSM_STATE), 0.3),
        'page_table': page_table,
        'c_prompt': normal((BATCH, D_MODEL)),
        'c_sample': normal((DEC_BATCH, D_MODEL)),
        'w_ada': normal((DEPTH, D_MODEL, 6 * D_MODEL), 0.3 * D_MODEL ** -0.5),
        'b_ada': normal((DEPTH, 6 * D_MODEL), 0.01),
        'g_mix': 1.0 + normal((DEPTH, D_MODEL), 0.02),
        'w_in': normal((DEPTH, D_MODEL, D_IN), D_MODEL ** -0.5),
        'conv_w': normal((DEPTH, CONV_WIDTH, D_CONV), CONV_WIDTH ** -0.5),
        'w_conv_out': normal((DEPTH, D_CONV, D_MODEL), D_CONV ** -0.5),
        'ssm_lam_re': -0.5 + normal((DEPTH, SSM_GROUPS, SSM_STATE), 0.01),
        'ssm_lam_im': lam_im0 + normal((DEPTH, SSM_GROUPS, SSM_STATE), 0.01),
        'ssm_log_step': jax.random.uniform(next(keys), (DEPTH, SSM_GROUPS), f32,
                                           minval=math.log(1e-3), maxval=math.log(1e-1)),
        'ssm_b_re': normal((DEPTH, SSM_GROUPS, SSM_STATE, SSM_GROUP), (2 * SSM_GROUP) ** -0.5),
        'ssm_b_im': normal((DEPTH, SSM_GROUPS, SSM_STATE, SSM_GROUP), (2 * SSM_GROUP) ** -0.5),
        'ssm_c_re': normal((DEPTH, SSM_GROUPS, SSM_GROUP, SSM_STATE), SSM_STATE ** -0.5),
        'ssm_c_im': normal((DEPTH, SSM_GROUPS, SSM_GROUP, SSM_STATE), SSM_STATE ** -0.5),
        'ssm_d': normal((DEPTH, D_SSM)),
        'w_glu_a': normal((DEPTH, D_SSM, D_MODEL), D_SSM ** -0.5),
        'w_glu_b': normal((DEPTH, D_SSM, D_MODEL), D_SSM ** -0.5),
        'w_o': normal((DEPTH, D_MODEL, D_MODEL), D_MODEL ** -0.5),
        'g_ffn': 1.0 + normal((DEPTH, D_MODEL), 0.02),
        'w_ff1': normal((DEPTH, D_MODEL, D_FF), D_MODEL ** -0.5),
        'w_ff3': normal((DEPTH, D_MODEL, D_FF), D_MODEL ** -0.5),
        'w_ff2': normal((DEPTH, D_FF, D_MODEL), D_FF ** -0.5),
        'g_final': 1.0 + normal((D_MODEL,), 0.02),
    }


def reference(x_prompt, x_sample, cache_k, cache_v, state_conv, state_ssm_re, state_ssm_im, page_table,
              c_prompt, c_sample, w_ada, b_ada, g_mix, w_in, conv_w, w_conv_out, ssm_lam_re, ssm_lam_im,
              ssm_log_step, ssm_b_re, ssm_b_im, ssm_c_re, ssm_c_im, ssm_d, w_glu_a, w_glu_b, w_o, g_ffn,
              w_ff1, w_ff3, w_ff2, g_final):
    params = dict(w_ada=w_ada, b_ada=b_ada, g_mix=g_mix, w_in=w_in, conv_w=conv_w, w_conv_out=w_conv_out,
                  ssm_lam_re=ssm_lam_re, ssm_lam_im=ssm_lam_im, ssm_log_step=ssm_log_step,
                  ssm_b_re=ssm_b_re, ssm_b_im=ssm_b_im, ssm_c_re=ssm_c_re, ssm_c_im=ssm_c_im, ssm_d=ssm_d,
                  w_glu_a=w_glu_a, w_glu_b=w_glu_b, w_o=w_o, g_ffn=g_ffn, w_ff1=w_ff1, w_ff3=w_ff3,
                  w_ff2=w_ff2, g_final=g_final)
    y_prompt, k_prompt, v_prompt, conv_prompt, ssm_re_prompt, ssm_im_prompt = trunk(x_prompt, c_prompt, params)
    y_sample, k_sample, v_sample, conv_sample, ssm_re_sample, ssm_im_sample = trunk(
        x_sample, c_sample, params, state_conv, state_ssm_re, state_ssm_im, cache_k, cache_v, page_table)
    return (y_prompt, y_sample, k_prompt, v_prompt, conv_prompt, ssm_re_prompt, ssm_im_prompt,
            k_sample, v_sample, conv_sample, ssm_re_sample, ssm_im_sample)
```

```python
import functools

import jax
import jax.numpy as jnp
from jax import lax
from jax.experimental import pallas as pl
from jax.experimental.pallas import tpu as pltpu

F32 = jnp.float32
BF16 = jnp.bfloat16

D_MODEL = 1024
DEPTH = 4
SEQ = 2048
BATCH = 8
DEC_BATCH = 128
PAGE_SIZE = 128
EPS = 1e-6
CONV_WIDTH = 3
SSM_GROUP = 16
SSM_GROUPS = D_MODEL // SSM_GROUP
SSM_STATE = 64
SSM_LANES = SSM_GROUPS * SSM_STATE
HEAD_DIM = 128
N_HEADS = D_MODEL // HEAD_DIM
N_KV_HEADS = N_HEADS // 2
KV_GROUP = N_HEADS // N_KV_HEADS
KV_DIM = N_KV_HEADS * HEAD_DIM
MOBA_BLOCK = 256
MOBA_TOPK = 3
ATTN_SCALE = HEAD_DIM ** -0.5
NEG = -1e30
D_FF = 2816
D_IN = 9216
COL_CB, COL_CC, COL_CX, COL_U, COL_Q = 0, 1, 2, 3, 4
COL_K = 5 * D_MODEL
COL_V = COL_K + KV_DIM
COL_GA, COL_GB, COL_GC = 6, 7, 8

LANES = 128
SUBLANES = 8
MXU_DIM = 256
VMEM_LIMIT = 56 * 1024 * 1024

S5_SLABS = D_MODEL // MXU_DIM
S5_SLAB_LANES = SSM_LANES // S5_SLABS


def _params(*sem):
    return pltpu.CompilerParams(dimension_semantics=sem, vmem_limit_bytes=VMEM_LIMIT)


def _ada_kernel(c_ref, w_ref, b_ref, o_ref):
    c = jax.nn.silu(c_ref[...]).astype(BF16)
    o_ref[...] = jnp.dot(c, w_ref[...].astype(BF16), preferred_element_type=F32) + b_ref[...]


def _ada(c_all, w_ada, b_ada):
    n_rows = c_all.shape[0]
    tn = 1536
    return pl.pallas_call(
        _ada_kernel,
        out_shape=jax.ShapeDtypeStruct((DEPTH, n_rows, 6 * D_MODEL), F32),
        grid=(DEPTH, 6 * D_MODEL // tn),
        in_specs=[
            pl.BlockSpec((n_rows, D_MODEL), lambda l, j: (0, 0)),
            pl.BlockSpec((None, D_MODEL, tn), lambda l, j: (l, 0, j)),
            pl.BlockSpec((None, 1, tn), lambda l, j: (l, 0, j)),
        ],
        out_specs=pl.BlockSpec((None, n_rows, tn), lambda l, j: (l, 0, j)),
        compiler_params=_params("parallel", "parallel"),
    )(c_all, w_ada, b_ada.reshape(DEPTH, 1, 6 * D_MODEL))


def _s5_disc_kernel(lre_ref, lim_ref, ls_ref, bre_ref, bim_ref, lbre_ref, lbim_ref, bbre_ref, bbim_ref):
    lre = lre_ref[...]
    lim = lim_ref[...]
    step = jnp.exp(ls_ref[...])
    mag = jnp.exp(lre * step)
    ang = lim * step
    lbre = mag * jnp.cos(ang)
    lbim = mag * jnp.sin(ang)
    lbre_ref[...] = lbre
    lbim_ref[...] = lbim
    nre = lbre - 1.0
    den = lre * lre + lim * lim
    cre = (nre * lre + lbim * lim) / den
    cim = (lbim * lre - nre * lim) / den
    bre = bre_ref[...]
    bim = bim_ref[...]
    bbre_ref[...] = cre * bre - cim * bim
    bbim_ref[...] = cre * bim + cim * bre


def _s5_discretize(lam_re, lam_im, log_step, b_re_t, b_im_t):
    g, p, i = SSM_GROUPS, SSM_STATE, SSM_GROUP
    gp = pl.BlockSpec((None, g, 1, p), lambda l: (l, 0, 0, 0))
    gip = pl.BlockSpec((None, g, i, p), lambda l: (l, 0, 0, 0))
    return pl.pallas_call(
        _s5_disc_kernel,
        out_shape=(jax.ShapeDtypeStruct((DEPTH, g, 1, p), F32), jax.ShapeDtypeStruct((DEPTH, g, 1, p), F32),
                   jax.ShapeDtypeStruct((DEPTH, g, i, p), F32), jax.ShapeDtypeStruct((DEPTH, g, i, p), F32)),
        grid=(DEPTH,),
        in_specs=[gp, gp, pl.BlockSpec((None, g, 1, 1), lambda l: (l, 0, 0, 0)), gip, gip],
        out_specs=(gp, gp, gip, gip),
        compiler_params=_params("parallel"),
    )(lam_re.reshape(DEPTH, g, 1, p), lam_im.reshape(DEPTH, g, 1, p), log_step.reshape(DEPTH, g, 1, 1),
      b_re_t, b_im_t)


def _block_diag_in(b_t):
    gs = SSM_GROUPS // S5_SLABS
    b = b_t.reshape(DEPTH, S5_SLABS, gs, SSM_GROUP, SSM_STATE)
    eye = jnp.eye(gs, dtype=b_t.dtype)
    out = b[:, :, :, :, None, :] * eye[None, None, :, None, :, None]
    return out.reshape(DEPTH, S5_SLABS, gs * SSM_GROUP, gs * SSM_STATE)


def _block_diag_out(c):
    gs = SSM_GROUPS // S5_SLABS
    ct = jnp.swapaxes(c, 2, 3).reshape(DEPTH, S5_SLABS, gs, SSM_STATE, SSM_GROUP)
    eye = jnp.eye(gs, dtype=c.dtype)
    out = ct[:, :, :, :, None, :] * eye[None, None, :, None, :, None]
    return out.reshape(DEPTH, S5_SLABS, gs * SSM_STATE, gs * SSM_GROUP)


def _mod_norm(x, g, scale, shift):
    y = x * lax.rsqrt(jnp.mean(x * x, axis=-1, keepdims=True) + EPS)
    return (y * g) * (1.0 + scale) + shift


def _in_proj_kernel(x_ref, g_ref, sc_ref, sh_ref, w_ref, o_ref, h_ref):
    @pl.when(pl.program_id(1) == 0)
    def _():
        h_ref[...] = _mod_norm(x_ref[...], g_ref[...], sc_ref[...], sh_ref[...]).astype(BF16)

    o_ref[...] = jnp.dot(h_ref[...], w_ref[...], preferred_element_type=F32)


def _mod_spec(mod, mod5, layer, chunk, tm, rows_per_seq, decode):
    if decode:
        return mod, pl.BlockSpec((None, tm, D_MODEL), lambda i, *_: (layer, 0, chunk))
    return mod5, pl.BlockSpec((None, None, None, 1, D_MODEL),
                              lambda i, *_: (layer, DEC_BATCH + (i * tm) // rows_per_seq, chunk, 0, 0))


def _in_proj(x, mods, layer, g_mix, w_in, decode):
    m = x.shape[0]
    tm = m if decode else 1024
    tn = 512
    sc_arr, sc_spec = _mod_spec(*mods, layer, 1, tm, SEQ, decode)
    sh_arr, sh_spec = _mod_spec(*mods, layer, 0, tm, SEQ, decode)
    return pl.pallas_call(
        _in_proj_kernel,
        out_shape=jax.ShapeDtypeStruct((m, D_IN), F32),
        grid=(m // tm, D_IN // tn),
        in_specs=[
            pl.BlockSpec((tm, D_MODEL), lambda i, j: (i, 0)),
            pl.BlockSpec((None, 1, D_MODEL), lambda i, j: (layer, 0, 0)),
            sc_spec, sh_spec,
            pl.BlockSpec((None, D_MODEL, tn), lambda i, j: (layer, 0, j)),
        ],
        out_specs=pl.BlockSpec((tm, tn), lambda i, j: (i, j)),
        scratch_shapes=[pltpu.VMEM((tm, D_MODEL), BF16)],
        compiler_params=_params("parallel", "arbitrary"),
    )(x, g_mix, sc_arr, sh_arr, w_in)


def _s5_readout(u_slab, sre_slab, sim_slab, cre_ref, cim_ref, d_ref, s):
    y = (jnp.dot(sre_slab.astype(BF16), cre_ref[s], preferred_element_type=F32)
         - jnp.dot(sim_slab.astype(BF16), cim_ref[s], preferred_element_type=F32))
    y = y + u_slab * d_ref[:, s * MXU_DIM:(s + 1) * MXU_DIM]
    return jax.nn.gelu(y, approximate=True)


def _glu(g, wa_ref, wb_ref):
    gb = g.astype(BF16)
    return (jnp.dot(gb, wa_ref[...], preferred_element_type=F32)
            * jax.nn.sigmoid(jnp.dot(gb, wb_ref[...], preferred_element_type=F32)))


SCAN_LANES = 512


def _s5_prompt_kernel(u_ref, lre_ref, lim_ref, bre_ref, bim_ref, cre_ref, cim_ref, d_ref, wa_ref, wb_ref,
                      ob_ref, fre_ref, fim_ref,
                      us_ref, sre_ref, sim_ref, os_ref, st_ref, *, tt):
    n_lane_slabs = D_MODEL // LANES

    @pl.when(pl.program_id(0) == 0)
    def _():
        st_ref[...] = jnp.zeros_like(st_ref)

    for b in range(BATCH):
        for sl in range(n_lane_slabs):
            us_ref[sl, pl.ds(b, tt, stride=BATCH), :] = u_ref[b, :, sl * LANES:(sl + 1) * LANES]

    def u_slab(s):
        per = MXU_DIM // LANES
        return jnp.concatenate([us_ref[per * s + k] for k in range(per)], axis=1)

    for s in range(S5_SLABS):
        ub = u_slab(s).astype(BF16)
        cols = slice(s * S5_SLAB_LANES, (s + 1) * S5_SLAB_LANES)
        sre_ref[:, cols] = jnp.dot(ub, bre_ref[s], preferred_element_type=F32)
        sim_ref[:, cols] = jnp.dot(ub, bim_ref[s], preferred_element_type=F32)

    for c in range(SSM_LANES // SCAN_LANES):
        cols = slice(c * SCAN_LANES, (c + 1) * SCAN_LANES)
        lre = jnp.broadcast_to(lre_ref[:, cols], (BATCH, SCAN_LANES))
        lim = jnp.broadcast_to(lim_ref[:, cols], (BATCH, SCAN_LANES))

        def step(t, carry, cols=cols, lre=lre, lim=lim):
            pre, pim = carry
            rows = pl.ds(pl.multiple_of(t * BATCH, BATCH), BATCH)
            nre = lre * pre - lim * pim + sre_ref[rows, cols]
            nim = lre * pim + lim * pre + sim_ref[rows, cols]
            sre_ref[rows, cols] = nre
            sim_ref[rows, cols] = nim
            return nre, nim

        fre, fim = lax.fori_loop(0, tt, step, (st_ref[0, :, cols], st_ref[1, :, cols]), unroll=4)
        st_ref[0, :, cols] = fre
        st_ref[1, :, cols] = fim

    fre_ref[...] = st_ref[0]
    fim_ref[...] = st_ref[1]

    g = jnp.concatenate(
        [_s5_readout(u_slab(s), sre_ref[:, s * S5_SLAB_LANES:(s + 1) * S5_SLAB_LANES],
                     sim_ref[:, s * S5_SLAB_LANES:(s + 1) * S5_SLAB_LANES], cre_ref, cim_ref, d_ref, s)
         for s in range(S5_SLABS)], axis=1)
    ob = _glu(g, wa_ref, wb_ref)

    for sl in range(n_lane_slabs):
        os_ref[sl] = ob[:, sl * LANES:(sl + 1) * LANES]
    for b in range(BATCH):
        for sl in range(n_lane_slabs):
            ob_ref[b, :, sl * LANES:(sl + 1) * LANES] = os_ref[sl, pl.ds(b, tt, stride=BATCH), :]


def _s5_decode_kernel(u_ref, s0re_ref, s0im_ref, lre_ref, lim_ref, bre_ref, bim_ref, cre_ref, cim_ref, d_ref,
                      wa_ref, wb_ref, ob_ref, fre_ref, fim_ref):
    gs = []
    for s in range(S5_SLABS):
        cols = slice(s * S5_SLAB_LANES, (s + 1) * S5_SLAB_LANES)
        u = u_ref[:, s * MXU_DIM:(s + 1) * MXU_DIM]
        ub = u.astype(BF16)
        lre, lim = lre_ref[:, cols], lim_ref[:, cols]
        pre, pim = s0re_ref[:, cols], s0im_ref[:, cols]
        nre = lre * pre - lim * pim + jnp.dot(ub, bre_ref[s], preferred_element_type=F32)
        nim = lre * pim + lim * pre + jnp.dot(ub, bim_ref[s], preferred_element_type=F32)
        fre_ref[:, cols] = nre
        fim_ref[:, cols] = nim
        gs.append(_s5_readout(u, nre, nim, cre_ref, cim_ref, d_ref, s))
    ob_ref[...] = _glu(jnp.concatenate(gs, axis=1), wa_ref, wb_ref)


def _s5_weight_specs(layer, n_grid):
    def const(shape, *idx):
        return pl.BlockSpec(shape, lambda *_: idx)

    return [
        const((None, 1, SSM_LANES), layer, 0, 0),
        const((None, 1, SSM_LANES), layer, 0, 0),
        const((None, S5_SLABS, MXU_DIM, S5_SLAB_LANES), layer, 0, 0, 0),
        const((None, S5_SLABS, MXU_DIM, S5_SLAB_LANES), layer, 0, 0, 0),
        const((None, S5_SLABS, S5_SLAB_LANES, MXU_DIM), layer, 0, 0, 0),
        const((None, S5_SLABS, S5_SLAB_LANES, MXU_DIM), layer, 0, 0, 0),
        const((None, 1, D_MODEL), layer, 0, 0),
        const((None, D_MODEL, D_MODEL), layer, 0, 0),
        const((None, D_MODEL, D_MODEL), layer, 0, 0),
    ]


def _s5_prompt(proj3, layer, s5w):
    tt = 32
    rows = tt * BATCH
    kern = functools.partial(_s5_prompt_kernel, tt=tt)
    state = jax.ShapeDtypeStruct((BATCH, SSM_LANES), F32)
    state_spec = pl.BlockSpec((BATCH, SSM_LANES), lambda i: (0, 0))
    return pl.pallas_call(
        kern,
        out_shape=(jax.ShapeDtypeStruct((BATCH, SEQ, D_MODEL), F32), state, state),
        grid=(SEQ // tt,),
        in_specs=[pl.BlockSpec((BATCH, tt, D_MODEL), lambda i: (0, i, COL_U))] + _s5_weight_specs(layer, 1),
        out_specs=(pl.BlockSpec((BATCH, tt, D_MODEL), lambda i: (0, i, 0)), state_spec, state_spec),
        scratch_shapes=[
            pltpu.VMEM((D_MODEL // LANES, rows, LANES), F32),
            pltpu.VMEM((rows, SSM_LANES), F32),
            pltpu.VMEM((rows, SSM_LANES), F32),
            pltpu.VMEM((D_MODEL // LANES, rows, LANES), F32),
            pltpu.VMEM((2, BATCH, SSM_LANES), F32),
        ],
        compiler_params=_params("arbitrary"),
    )(proj3, *s5w)


def _s5_decode(proj, s0re, s0im, layer, s5w):
    m = proj.shape[0]
    state = jax.ShapeDtypeStruct((m, SSM_LANES), F32)
    full = pl.BlockSpec((m, SSM_LANES), lambda i: (0, 0))
    return pl.pallas_call(
        _s5_decode_kernel,
        out_shape=(jax.ShapeDtypeStruct((m, D_MODEL), F32), state, state),
        grid=(1,),
        in_specs=[pl.BlockSpec((m, D_MODEL), lambda i: (0, COL_U)),
                  pl.BlockSpec((None, m, SSM_LANES), lambda i: (layer, 0, 0)),
                  pl.BlockSpec((None, m, SSM_LANES), lambda i: (layer, 0, 0))] + _s5_weight_specs(layer, 1),
        out_specs=(pl.BlockSpec((m, D_MODEL), lambda i: (0, 0)), full, full),
        compiler_params=_params("arbitrary"),
    )(proj, s0re, s0im, *s5w)


def _topk_select(gate, n_valid, lane):
    g = jnp.where(lane < n_valid, gate, NEG)
    rank = jnp.zeros(gate.shape, jnp.int32)
    for jp in range(n_valid):
        col = g[:, jp:jp + 1]
        rank = rank + jnp.where(col > g, 1, jnp.where(col == g, jnp.where(lane > jp, 1, 0), 0))
    return jnp.where(lane < n_valid, jnp.where(rank < MOBA_TOPK, 1.0, 0.0), 0.0)


def _moba_prompt_kernel(q_ref, k_ref, v_ref, o_ref, ko_ref, vo_ref):
    nb = SEQ // MOBA_BLOCK
    k = k_ref[...]
    v = v_ref[...]
    ko_ref[...] = k
    vo_ref[...] = v
    kb = k.astype(BF16)
    vb = v.astype(BF16)
    kmean = jnp.mean(k.reshape(nb, MOBA_BLOCK, HEAD_DIM), axis=1).astype(BF16)
    rows = KV_GROUP * MOBA_BLOCK
    lane = lax.broadcasted_iota(jnp.int32, (rows, nb), 1)
    q_in_blk = lax.broadcasted_iota(jnp.int32, (rows, MOBA_BLOCK), 0) & (MOBA_BLOCK - 1)
    k_in_blk = lax.broadcasted_iota(jnp.int32, (rows, MOBA_BLOCK), 1)
    causal = k_in_blk <= q_in_blk
    nt = (((1,), (1,)), ((), ()))
    for i in range(nb):
        q2 = q_ref[i * MOBA_BLOCK:(i + 1) * MOBA_BLOCK, :]
        q = jnp.concatenate([q2[:, h * HEAD_DIM:(h + 1) * HEAD_DIM] for h in range(KV_GROUP)], axis=0)
        nk = (i + 1) * MOBA_BLOCK
        s = lax.dot_general((q * ATTN_SCALE).astype(BF16), kb[:nk], nt, preferred_element_type=F32)
        parts = []
        if i > 0:
            gate = lax.dot_general(q.astype(BF16), kmean, nt, preferred_element_type=F32)
            sel = _topk_select(gate, i, lane)
            for j in range(i):
                parts.append(jnp.where(sel[:, j:j + 1] > 0.0, s[:, j * MOBA_BLOCK:(j + 1) * MOBA_BLOCK], NEG))
        parts.append(jnp.where(causal, s[:, i * MOBA_BLOCK:], NEG))
        s = jnp.concatenate(parts, axis=1) if len(parts) > 1 else parts[0]
        m = jnp.max(s, axis=-1, keepdims=True)
        p = jnp.exp(s - m)
        l = jnp.sum(p, axis=-1, keepdims=True)
        o = jnp.dot(p.astype(BF16), vb[:nk], preferred_element_type=F32) / l
        for h in range(KV_GROUP):
            o_ref[i * MOBA_BLOCK:(i + 1) * MOBA_BLOCK, h * HEAD_DIM:(h + 1) * HEAD_DIM] = (
                o[h * MOBA_BLOCK:(h + 1) * MOBA_BLOCK])


def _moba_prompt(proj3):
    gw = KV_GROUP * HEAD_DIM
    kv = jax.ShapeDtypeStruct((BATCH, SEQ, KV_DIM), F32)
    kv_spec = pl.BlockSpec((None, SEQ, HEAD_DIM), lambda b, h: (b, 0, h))
    return pl.pallas_call(
        _moba_prompt_kernel,
        out_shape=(jax.ShapeDtypeStruct((BATCH, SEQ, D_MODEL), F32), kv, kv),
        grid=(BATCH, N_KV_HEADS),
        in_specs=[
            pl.BlockSpec((None, SEQ, gw), lambda b, h: (b, 0, COL_Q * D_MODEL // gw + h)),
            pl.BlockSpec((None, SEQ, HEAD_DIM), lambda b, h: (b, 0, COL_K // HEAD_DIM + h)),
            pl.BlockSpec((None, SEQ, HEAD_DIM), lambda b, h: (b, 0, COL_V // HEAD_DIM + h)),
        ],
        out_specs=(pl.BlockSpec((None, SEQ, gw), lambda b, h: (b, 0, h)), kv_spec, kv_spec),
        compiler_params=_params("parallel", "parallel"),
    )(proj3, proj3, proj3)


def _moba_decode_kernel(pt_ref, q_ref, kn_ref, vn_ref, *refs, n_pages):
    del pt_ref
    k_pages = refs[:n_pages]
    v_pages = refs[n_pages:2 * n_pages]
    o_ref = refs[2 * n_pages]
    per = MOBA_BLOCK // PAGE_SIZE
    nb = n_pages // per
    nt = (((1,), (1,)), ((), ()))
    q = q_ref[...]
    zero = jnp.zeros((1, HEAD_DIM), F32)
    qbd = jnp.concatenate(
        [jnp.concatenate([q[:, h * HEAD_DIM:(h + 1) * HEAD_DIM] if kk == h // KV_GROUP else zero
                          for kk in range(N_KV_HEADS)], axis=1) for h in range(N_HEADS)], axis=0)
    qs = (qbd * ATTN_SCALE).astype(BF16)
    scores, kmeans = [], []
    for j in range(nb):
        kj = jnp.concatenate([k_pages[per * j + r][...] for r in range(per)], axis=0)
        kmeans.append(jnp.mean(kj, axis=0, keepdims=True))
        scores.append(lax.dot_general(qs, kj.astype(BF16), nt, preferred_element_type=F32))
    kmean = jnp.concatenate(kmeans, axis=0).astype(BF16)
    gate = lax.dot_general(qbd.astype(BF16), kmean, nt, preferred_element_type=F32)
    lane = lax.broadcasted_iota(jnp.int32, (N_HEADS, nb), 1)
    sel = _topk_select(gate, nb, lane)
    s_past = jnp.concatenate([jnp.where(sel[:, j:j + 1] > 0.0, scores[j], NEG) for j in range(nb)], axis=1)
    s_own = jnp.sum(qbd * ATTN_SCALE * kn_ref[...], axis=-1, keepdims=True)
    m = jnp.maximum(jnp.max(s_past, axis=-1, keepdims=True), s_own)
    p = jnp.exp(s_past - m)
    p_own = jnp.exp(s_own - m)
    l = jnp.sum(p, axis=-1, keepdims=True) + p_own
    pb = p.astype(BF16)
    acc = p_own * vn_ref[...]
    for j in range(nb):
        vj = jnp.concatenate([v_pages[per * j + r][...] for r in range(per)], axis=0).astype(BF16)
        acc = acc + jnp.dot(pb[:, j * MOBA_BLOCK:(j + 1) * MOBA_BLOCK], vj, preferred_element_type=F32)
    acc = acc / l
    o_ref[...] = jnp.concatenate(
        [acc[h:h + 1, (h // KV_GROUP) * HEAD_DIM:(h // KV_GROUP + 1) * HEAD_DIM] for h in range(N_HEADS)], axis=1)


def _moba_decode(proj, cache_k, cache_v, page_table, layer):
    m = proj.shape[0]
    n_pages = page_table.shape[1]
    proj3 = proj.reshape(m, 1, D_IN)

    def page_spec(pg):
        return pl.BlockSpec((None, None, PAGE_SIZE, KV_DIM), lambda b, pt: (layer, pt[b, pg], 0, 0))

    grid_spec = pltpu.PrefetchScalarGridSpec(
        num_scalar_prefetch=1,
        grid=(m,),
        in_specs=[
            pl.BlockSpec((None, 1, D_MODEL), lambda b, pt: (b, 0, COL_Q)),
            pl.BlockSpec((None, 1, KV_DIM), lambda b, pt: (b, 0, COL_K // KV_DIM)),
            pl.BlockSpec((None, 1, KV_DIM), lambda b, pt: (b, 0, COL_V // KV_DIM)),
        ] + [page_spec(pg) for pg in range(n_pages)] + [page_spec(pg) for pg in range(n_pages)],
        out_specs=pl.BlockSpec((None, 1, D_MODEL), lambda b, pt: (b, 0, 0)),
    )
    out = pl.pallas_call(
        functools.partial(_moba_decode_kernel, n_pages=n_pages),
        out_shape=jax.ShapeDtypeStruct((m, 1, D_MODEL), F32),
        grid_spec=grid_spec,
        compiler_params=_params("parallel"),
    )(page_table, proj3, proj3, proj3, *([cache_k] * n_pages), *([cache_v] * n_pages))
    return out.reshape(m, D_MODEL)


def _merge_tail(x, gate, cb, zc, ga, gb, gc, ob, oc, wc_ref, wo_ref):
    out_a = jnp.dot((cb * zc).astype(BF16), wc_ref[...], preferred_element_type=F32)
    merged = jax.nn.sigmoid(ga) * out_a + jax.nn.sigmoid(gb) * ob + jax.nn.sigmoid(gc) * oc
    return x + gate * jnp.dot(merged.astype(BF16), wo_ref[...], preferred_element_type=F32)


def _mix_prompt_kernel(x_ref, gate_ref, cb_ref, cc_ref, cx_ref, ga_ref, gb_ref, gc_ref, ob_ref, oc_ref,
                       cw_ref, wc_ref, wo_ref, xo_ref, last_ref, z_ref, *, tm, tiles_per_seq):
    tail = SUBLANES

    @pl.when(pl.program_id(0) % tiles_per_seq == 0)
    def _():
        z_ref[0:tail, :] = jnp.zeros((tail, D_MODEL), F32)

    z = cc_ref[...] * cx_ref[...]
    z_ref[tail:tail + tm, :] = z
    w = cw_ref[...]
    zc = (z_ref[tail - 2:tail - 2 + tm, :] * w[0:1] + z_ref[tail - 1:tail - 1 + tm, :] * w[1:2] + z * w[2:3])
    z_last = z[tm - tail:, :]
    z_ref[0:tail, :] = z_last
    last_ref[...] = z_last
    xo_ref[...] = _merge_tail(x_ref[...], gate_ref[...], cb_ref[...], zc, ga_ref[...], gb_ref[...], gc_ref[...],
                              ob_ref[...], oc_ref[...], wc_ref, wo_ref)


def _mix_decode_kernel(x_ref, gate_ref, cb_ref, cc_ref, cx_ref, ga_ref, gb_ref, gc_ref, ob_ref, oc_ref,
                       cw_ref, wc_ref, wo_ref, p0_ref, p1_ref, xo_ref, last_ref):
    z = cc_ref[...] * cx_ref[...]
    w = cw_ref[...]
    p1 = p1_ref[...]
    zc = p0_ref[...] * w[0:1] + p1 * w[1:2] + z * w[2:3]
    last_ref[:, 0:D_MODEL] = p1
    last_ref[:, D_MODEL:2 * D_MODEL] = z
    xo_ref[...] = _merge_tail(x_ref[...], gate_ref[...], cb_ref[...], zc, ga_ref[...], gb_ref[...], gc_ref[...],
                              ob_ref[...], oc_ref[...], wc_ref, wo_ref)


def _mix(x, mods, layer, proj, out_b, out_c, conv_w, w_conv_out, w_o, decode, conv_state=None):
    m = x.shape[0]
    tm = m if decode else 256
    gate_arr, gate_spec = _mod_spec(*mods, layer, 2, tm, SEQ, decode)

    def col(c):
        return pl.BlockSpec((tm, D_MODEL), lambda i: (i, c))

    row = pl.BlockSpec((tm, D_MODEL), lambda i: (i, 0))
    wspec = pl.BlockSpec((None, D_MODEL, D_MODEL), lambda i: (layer, 0, 0))
    in_specs = [row, gate_spec, col(COL_CB), col(COL_CC), col(COL_CX), col(COL_GA), col(COL_GB), col(COL_GC),
                row, row, pl.BlockSpec((None, CONV_WIDTH, D_MODEL), lambda i: (layer, 0, 0)), wspec, wspec]
    args = [x, gate_arr, proj, proj, proj, proj, proj, proj, out_b, out_c, conv_w, w_conv_out, w_o]
    if decode:
        state2 = conv_state.reshape(DEPTH, m, (CONV_WIDTH - 1) * D_MODEL)
        in_specs += [pl.BlockSpec((None, m, D_MODEL), lambda i: (layer, 0, 0)),
                     pl.BlockSpec((None, m, D_MODEL), lambda i: (layer, 0, 1))]
        args += [state2, state2]
        return pl.pallas_call(
            _mix_decode_kernel,
            out_shape=(jax.ShapeDtypeStruct((m, D_MODEL), F32),
                       jax.ShapeDtypeStruct((m, (CONV_WIDTH - 1) * D_MODEL), F32)),
            grid=(1,),
            in_specs=in_specs,
            out_specs=(row, pl.BlockSpec((m, (CONV_WIDTH - 1) * D_MODEL), lambda i: (0, 0))),
            compiler_params=_params("arbitrary"),
        )(*args)
    tiles_per_seq = SEQ // tm
    return pl.pallas_call(
        functools.partial(_mix_prompt_kernel, tm=tm, tiles_per_seq=tiles_per_seq),
        out_shape=(jax.ShapeDtypeStruct((m, D_MODEL), F32),
                   jax.ShapeDtypeStruct((BATCH, SUBLANES, D_MODEL), F32)),
        grid=(m // tm,),
        in_specs=in_specs,
        out_specs=(row, pl.BlockSpec((None, SUBLANES, D_MODEL), lambda i: (i // tiles_per_seq, 0, 0))),
        scratch_shapes=[pltpu.VMEM((SUBLANES + tm, D_MODEL), F32)],
        compiler_params=_params("arbitrary"),
    )(*args)


def _ffn_kernel(x_ref, g_ref, sc_ref, sh_ref, gate_ref, w1_ref, w3_ref, w2_ref, o_ref, h_ref, acc_ref):
    f = pl.program_id(1)

    @pl.when(f == 0)
    def _():
        h_ref[...] = _mod_norm(x_ref[...], g_ref[...], sc_ref[...], sh_ref[...]).astype(BF16)
        acc_ref[...] = jnp.zeros_like(acc_ref)

    h = h_ref[...]
    a = (jax.nn.silu(jnp.dot(h, w1_ref[...], preferred_element_type=F32))
         * jnp.dot(h, w3_ref[...], preferred_element_type=F32))
    acc_ref[...] += jnp.dot(a.astype(BF16), w2_ref[...], preferred_element_type=F32)

    @pl.when(f == pl.num_programs(1) - 1)
    def _():
        o_ref[...] = x_ref[...] + gate_ref[...] * acc_ref[...]


def _ffn(x, mods, layer, g_ffn, w1, w3, w2, decode):
    m = x.shape[0]
    tm = m if decode else 512
    fc = D_FF // 2
    sc_arr, sc_spec = _mod_spec(*mods, layer, 4, tm, SEQ, decode)
    sh_arr, sh_spec = _mod_spec(*mods, layer, 3, tm, SEQ, decode)
    gt_arr, gt_spec = _mod_spec(*mods, layer, 5, tm, SEQ, decode)
    row = pl.BlockSpec((tm, D_MODEL), lambda i, f: (i, 0))
    return pl.pallas_call(
        _ffn_kernel,
        out_shape=jax.ShapeDtypeStruct((m, D_MODEL), F32),
        grid=(m // tm, D_FF // fc),
        in_specs=[row, pl.BlockSpec((None, 1, D_MODEL), lambda i, f: (layer, 0, 0)), sc_spec, sh_spec, gt_spec,
                  pl.BlockSpec((None, D_MODEL, fc), lambda i, f: (layer, 0, f)),
                  pl.BlockSpec((None, D_MODEL, fc), lambda i, f: (layer, 0, f)),
                  pl.BlockSpec((None, fc, D_MODEL), lambda i, f: (layer, f, 0))],
        out_specs=row,
        scratch_shapes=[pltpu.VMEM((tm, D_MODEL), BF16), pltpu.VMEM((tm, D_MODEL), F32)],
        compiler_params=_params("parallel", "arbitrary"),
    )(x, g_ffn, sc_arr, sh_arr, gt_arr, w1, w3, w2)


def _final_norm_kernel(x_ref, g_ref, o_ref):
    x = x_ref[...]
    o_ref[...] = x * lax.rsqrt(jnp.mean(x * x, axis=-1, keepdims=True) + EPS) * g_ref[...]


def _final_norm(x, g):
    m = x.shape[0]
    tm = min(m, 1024)
    row = pl.BlockSpec((tm, D_MODEL), lambda i: (i, 0))
    return pl.pallas_call(
        _final_norm_kernel,
        out_shape=jax.ShapeDtypeStruct((m, D_MODEL), F32),
        grid=(m // tm,),
        in_specs=[row, pl.BlockSpec((1, D_MODEL), lambda i: (0, 0))],
        out_specs=row,
        compiler_params=_params("parallel"),
    )(x, g.reshape(1, D_MODEL))


def _trunk(x, mods, w, decode, state=None):
    ks, vs, convs, res, ims = [], [], [], [], []
    for l in range(DEPTH):
        proj = _in_proj(x, mods, l, w["g_mix"], w["w_in"], decode)
        if decode:
            out_b, s_re, s_im = _s5_decode(proj, state["ssm_re"], state["ssm_im"], l, w["s5"])
            out_c = _moba_decode(proj, state["cache_k"], state["cache_v"], state["page_table"], l)
            k_new = proj[:, COL_K:COL_K + KV_DIM]
            v_new = proj[:, COL_V:COL_V + KV_DIM]
            x, conv_last = _mix(x, mods, l, proj, out_b, out_c, w["conv_w"], w["w_conv_out"], w["w_o"], True,
                                state["conv"])
            conv_last = conv_last.reshape(DEC_BATCH, CONV_WIDTH - 1, D_MODEL)
        else:
            proj3 = proj.reshape(BATCH, SEQ, D_IN)
            out_b, s_re, s_im = _s5_prompt(proj3, l, w["s5"])
            out_c, k_new, v_new = _moba_prompt(proj3)
            x, conv_last = _mix(x, mods, l, proj, out_b.reshape(BATCH * SEQ, D_MODEL),
                                out_c.reshape(BATCH * SEQ, D_MODEL), w["conv_w"], w["w_conv_out"], w["w_o"], False)
            conv_last = conv_last[:, SUBLANES - (CONV_WIDTH - 1):, :]
        x = _ffn(x, mods, l, w["g_ffn"], w["w_ff1"], w["w_ff3"], w["w_ff2"], decode)
        ks.append(k_new)
        vs.append(v_new)
        convs.append(conv_last)
        res.append(s_re)
        ims.append(s_im)
    y = _final_norm(x, w["g_final"])
    return y, jnp.stack(ks), jnp.stack(vs), jnp.stack(convs), jnp.stack(res), jnp.stack(ims)


def kernel(x_prompt, x_sample, cache_k, cache_v, state_conv, state_ssm_re, state_ssm_im, page_table, c_prompt, c_sample, w_ada, b_ada, g_mix, w_in, conv_w, w_conv_out, ssm_lam_re, ssm_lam_im, ssm_log_step, ssm_b_re, ssm_b_im, ssm_c_re, ssm_c_im, ssm_d, w_glu_a, w_glu_b, w_o, g_ffn, w_ff1, w_ff3, w_ff2, g_final):
    n_pool = cache_k.shape[1]
    mod = _ada(jnp.concatenate([c_sample, c_prompt], axis=0), w_ada, b_ada)
    mods = (mod, mod.reshape(DEPTH, DEC_BATCH + BATCH, 6, 1, D_MODEL))

    lam_re, lam_im, bb_re, bb_im = _s5_discretize(
        ssm_lam_re, ssm_lam_im, ssm_log_step, jnp.swapaxes(ssm_b_re, 2, 3), jnp.swapaxes(ssm_b_im, 2, 3))
    s5w = (lam_re.reshape(DEPTH, 1, SSM_LANES), lam_im.reshape(DEPTH, 1, SSM_LANES),
           _block_diag_in(bb_re).astype(BF16), _block_diag_in(bb_im).astype(BF16),
           _block_diag_out(ssm_c_re).astype(BF16), _block_diag_out(ssm_c_im).astype(BF16),
           ssm_d.reshape(DEPTH, 1, D_MODEL), w_glu_a.astype(BF16), w_glu_b.astype(BF16))
    w = dict(g_mix=g_mix.reshape(DEPTH, 1, D_MODEL), w_in=w_in.astype(BF16), s5=s5w, conv_w=conv_w,
             w_conv_out=w_conv_out.astype(BF16), w_o=w_o.astype(BF16), g_ffn=g_ffn.reshape(DEPTH, 1, D_MODEL),
             w_ff1=w_ff1.astype(BF16), w_ff3=w_ff3.astype(BF16), w_ff2=w_ff2.astype(BF16), g_final=g_final)

    yp, kp, vp, cp, rp, ip = _trunk(x_prompt.reshape(BATCH * SEQ, D_MODEL), mods, w, False)
    state = dict(ssm_re=state_ssm_re.reshape(DEPTH, DEC_BATCH, SSM_LANES),
                 ssm_im=state_ssm_im.reshape(DEPTH, DEC_BATCH, SSM_LANES),
                 cache_k=cache_k.reshape(DEPTH, n_pool, PAGE_SIZE, KV_DIM),
                 cache_v=cache_v.reshape(DEPTH, n_pool, PAGE_SIZE, KV_DIM),
                 page_table=page_table, conv=state_conv)
    ys, ksm, vsm, cs, rs, is_ = _trunk(x_sample.reshape(DEC_BATCH, D_MODEL), mods, w, True, state)

    kv_p = (DEPTH, BATCH, SEQ, N_KV_HEADS, HEAD_DIM)
    kv_s = (DEPTH, DEC_BATCH, 1, N_KV_HEADS, HEAD_DIM)
    st_p = (DEPTH, BATCH, SSM_GROUPS, SSM_STATE)
    st_s = (DEPTH, DEC_BATCH, SSM_GROUPS, SSM_STATE)
    return (yp.reshape(BATCH, SEQ, D_MODEL), ys.reshape(DEC_BATCH, 1, D_MODEL),
            kp.reshape(kv_p), vp.reshape(kv_p), cp, rp.reshape(st_p), ip.reshape(st_p),
            ksm.reshape(kv_s), vsm.reshape(kv_s), cs, rs.reshape(st_s), is_.reshape(st_s))
```

```python
import functools

import jax
import jax.numpy as jnp
from jax import lax
from jax.experimental import pallas as pl
from jax.experimental.pallas import tpu as pltpu

F32 = jnp.float32
BF16 = jnp.bfloat16

D_MODEL = 1024
DEPTH = 4
SEQ = 2048
BATCH = 8
DEC_BATCH = 128
PAGE_SIZE = 128
EPS = 1e-6
CONV_WIDTH = 3
SSM_GROUP = 16
SSM_GROUPS = D_MODEL // SSM_GROUP
SSM_STATE = 64
SSM_LANES = SSM_GROUPS * SSM_STATE
HEAD_DIM = 128
N_HEADS = D_MODEL // HEAD_DIM
N_KV_HEADS = N_HEADS // 2
KV_GROUP = N_HEADS // N_KV_HEADS
KV_DIM = N_KV_HEADS * HEAD_DIM
KV_GROUP_SHIFT = KV_GROUP.bit_length() - 1
assert KV_GROUP == 1 << KV_GROUP_SHIFT and N_KV_HEADS & (N_KV_HEADS - 1) == 0
PAGE_ROWS = PAGE_SIZE * N_KV_HEADS
MOBA_BLOCK = 256
MOBA_TOPK = 3
ATTN_SCALE = HEAD_DIM ** -0.5
NEG = -1e30
D_FF = 2816
D_IN = 9216
COL_CB, COL_CC, COL_CX, COL_U, COL_Q = 0, 1, 2, 3, 4
COL_K = 5 * D_MODEL
COL_V = COL_K + KV_DIM
COL_GA, COL_GB, COL_GC = 6, 7, 8

LANES = 128
SUBLANES = 8
MXU_DIM = 256
VMEM_LIMIT = 56 * 1024 * 1024

S5_SLABS = D_MODEL // MXU_DIM
S5_SLAB_LANES = SSM_LANES // S5_SLABS


def _params(*sem):
    return pltpu.CompilerParams(dimension_semantics=sem, vmem_limit_bytes=VMEM_LIMIT)


def _ada_kernel(c_ref, w_ref, b_ref, o_ref):
    c = jax.nn.silu(c_ref[...]).astype(BF16)
    o_ref[...] = jnp.dot(c, w_ref[...].astype(BF16), preferred_element_type=F32) + b_ref[...]


def _ada(c_all, w_ada, b_ada):
    n_rows = c_all.shape[0]
    tn = 1536
    return pl.pallas_call(
        _ada_kernel,
        out_shape=jax.ShapeDtypeStruct((DEPTH, n_rows, 6 * D_MODEL), F32),
        grid=(DEPTH, 6 * D_MODEL // tn),
        in_specs=[
            pl.BlockSpec((n_rows, D_MODEL), lambda l, j: (0, 0)),
            pl.BlockSpec((None, D_MODEL, tn), lambda l, j: (l, 0, j)),
            pl.BlockSpec((None, 1, tn), lambda l, j: (l, 0, j)),
        ],
        out_specs=pl.BlockSpec((None, n_rows, tn), lambda l, j: (l, 0, j)),
        compiler_params=_params("parallel", "parallel"),
    )(c_all, w_ada, b_ada.reshape(DEPTH, 1, 6 * D_MODEL))


def _s5_disc_kernel(lre_ref, lim_ref, ls_ref, bre_ref, bim_ref, lbre_ref, lbim_ref, bbre_ref, bbim_ref):
    lre = lre_ref[...]
    lim = lim_ref[...]
    step = jnp.exp(ls_ref[...])
    mag = jnp.exp(lre * step)
    ang = lim * step
    lbre = mag * jnp.cos(ang)
    lbim = mag * jnp.sin(ang)
    lbre_ref[...] = lbre
    lbim_ref[...] = lbim
    nre = lbre - 1.0
    den = lre * lre + lim * lim
    cre = (nre * lre + lbim * lim) / den
    cim = (lbim * lre - nre * lim) / den
    bre = bre_ref[...]
    bim = bim_ref[...]
    bbre_ref[...] = cre * bre - cim * bim
    bbim_ref[...] = cre * bim + cim * bre


def _s5_discretize(lam_re, lam_im, log_step, b_re_t, b_im_t):
    g, p, i = SSM_GROUPS, SSM_STATE, SSM_GROUP
    gp = pl.BlockSpec((None, g, 1, p), lambda l: (l, 0, 0, 0))
    gip = pl.BlockSpec((None, g, i, p), lambda l: (l, 0, 0, 0))
    return pl.pallas_call(
        _s5_disc_kernel,
        out_shape=(jax.ShapeDtypeStruct((DEPTH, g, 1, p), F32), jax.ShapeDtypeStruct((DEPTH, g, 1, p), F32),
                   jax.ShapeDtypeStruct((DEPTH, g, i, p), F32), jax.ShapeDtypeStruct((DEPTH, g, i, p), F32)),
        grid=(DEPTH,),
        in_specs=[gp, gp, pl.BlockSpec((None, g, 1, 1), lambda l: (l, 0, 0, 0)), gip, gip],
        out_specs=(gp, gp, gip, gip),
        compiler_params=_params("parallel"),
    )(lam_re.reshape(DEPTH, g, 1, p), lam_im.reshape(DEPTH, g, 1, p), log_step.reshape(DEPTH, g, 1, 1),
      b_re_t, b_im_t)


def _block_diag_in(b_t):
    gs = SSM_GROUPS // S5_SLABS
    b = b_t.reshape(DEPTH, S5_SLABS, gs, SSM_GROUP, SSM_STATE)
    eye = jnp.eye(gs, dtype=b_t.dtype)
    out = b[:, :, :, :, None, :] * eye[None, None, :, None, :, None]
    return out.reshape(DEPTH, S5_SLABS, gs * SSM_GROUP, gs * SSM_STATE)


def _block_diag_out(c):
    gs = SSM_GROUPS // S5_SLABS
    ct = jnp.swapaxes(c, 2, 3).reshape(DEPTH, S5_SLABS, gs, SSM_STATE, SSM_GROUP)
    eye = jnp.eye(gs, dtype=c.dtype)
    out = ct[:, :, :, :, None, :] * eye[None, None, :, None, :, None]
    return out.reshape(DEPTH, S5_SLABS, gs * SSM_STATE, gs * SSM_GROUP)


def _mod_norm(x, g, scale, shift):
    y = x * lax.rsqrt(jnp.mean(x * x, axis=-1, keepdims=True) + EPS)
    return (y * g) * (1.0 + scale) + shift


def _in_proj_kernel(x_ref, g_ref, sc_ref, sh_ref, w_ref, o_ref, h_ref):
    @pl.when(pl.program_id(1) == 0)
    def _():
        h_ref[...] = _mod_norm(x_ref[...], g_ref[...], sc_ref[...], sh_ref[...]).astype(BF16)

    o_ref[...] = jnp.dot(h_ref[...], w_ref[...], preferred_element_type=F32)


def _mod_spec(mod, mod5, layer, chunk, tm, rows_per_seq, decode):
    if decode:
        return mod, pl.BlockSpec((None, tm, D_MODEL), lambda i, *_: (layer, 0, chunk))
    return mod5, pl.BlockSpec((None, None, None, 1, D_MODEL),
                              lambda i, *_: (layer, DEC_BATCH + (i * tm) // rows_per_seq, chunk, 0, 0))


def _in_proj(x, mods, layer, g_mix, w_in, decode):
    m = x.shape[0]
    tm = m if decode else 2048
    tn = 512
    sc_arr, sc_spec = _mod_spec(*mods, layer, 1, tm, SEQ, decode)
    sh_arr, sh_spec = _mod_spec(*mods, layer, 0, tm, SEQ, decode)
    return pl.pallas_call(
        _in_proj_kernel,
        out_shape=jax.ShapeDtypeStruct((m, D_IN), F32),
        grid=(m // tm, D_IN // tn),
        in_specs=[
            pl.BlockSpec((tm, D_MODEL), lambda i, j: (i, 0)),
            pl.BlockSpec((None, 1, D_MODEL), lambda i, j: (layer, 0, 0)),
            sc_spec, sh_spec,
            pl.BlockSpec((None, D_MODEL, tn), lambda i, j: (layer, 0, j)),
        ],
        out_specs=pl.BlockSpec((tm, tn), lambda i, j: (i, j)),
        scratch_shapes=[pltpu.VMEM((tm, D_MODEL), BF16)],
        compiler_params=_params("parallel", "arbitrary"),
    )(x, g_mix, sc_arr, sh_arr, w_in)


def _s5_readout(u_slab, sre_slab, sim_slab, cre_ref, cim_ref, d_ref, s):
    y = (jnp.dot(sre_slab.astype(BF16), cre_ref[s], preferred_element_type=F32)
         - jnp.dot(sim_slab.astype(BF16), cim_ref[s], preferred_element_type=F32))
    y = y + u_slab * d_ref[:, s * MXU_DIM:(s + 1) * MXU_DIM]
    return jax.nn.gelu(y, approximate=True)


def _glu(g, wa_ref, wb_ref):
    gb = g.astype(BF16)
    return (jnp.dot(gb, wa_ref[...], preferred_element_type=F32)
            * jax.nn.sigmoid(jnp.dot(gb, wb_ref[...], preferred_element_type=F32)))


SCAN_LANES = 512


def _s5_prompt_kernel(u_ref, lre_ref, lim_ref, bre_ref, bim_ref, cre_ref, cim_ref, d_ref, wa_ref, wb_ref,
                      ob_ref, fre_ref, fim_ref,
                      us_ref, sre_ref, sim_ref, os_ref, st_ref, *, tt):
    n_lane_slabs = D_MODEL // LANES

    @pl.when(pl.program_id(0) == 0)
    def _():
        st_ref[...] = jnp.zeros_like(st_ref)

    for b in range(BATCH):
        for sl in range(n_lane_slabs):
            us_ref[sl, pl.ds(b, tt, stride=BATCH), :] = u_ref[b, :, sl * LANES:(sl + 1) * LANES]

    def u_slab(s):
        per = MXU_DIM // LANES
        return jnp.concatenate([us_ref[per * s + k] for k in range(per)], axis=1)

    for s in range(S5_SLABS):
        ub = u_slab(s).astype(BF16)
        cols = slice(s * S5_SLAB_LANES, (s + 1) * S5_SLAB_LANES)
        sre_ref[:, cols] = jnp.dot(ub, bre_ref[s], preferred_element_type=F32)
        sim_ref[:, cols] = jnp.dot(ub, bim_ref[s], preferred_element_type=F32)

    for c in range(SSM_LANES // SCAN_LANES):
        cols = slice(c * SCAN_LANES, (c + 1) * SCAN_LANES)
        lre = jnp.broadcast_to(lre_ref[:, cols], (BATCH, SCAN_LANES))
        lim = jnp.broadcast_to(lim_ref[:, cols], (BATCH, SCAN_LANES))

        def step(t, carry, cols=cols, lre=lre, lim=lim):
            pre, pim = carry
            rows = pl.ds(pl.multiple_of(t * BATCH, BATCH), BATCH)
            nre = lre * pre - lim * pim + sre_ref[rows, cols]
            nim = lre * pim + lim * pre + sim_ref[rows, cols]
            sre_ref[rows, cols] = nre
            sim_ref[rows, cols] = nim
            return nre, nim

        fre, fim = lax.fori_loop(0, tt, step, (st_ref[0, :, cols], st_ref[1, :, cols]), unroll=4)
        st_ref[0, :, cols] = fre
        st_ref[1, :, cols] = fim

    fre_ref[...] = st_ref[0]
    fim_ref[...] = st_ref[1]

    g = jnp.concatenate(
        [_s5_readout(u_slab(s), sre_ref[:, s * S5_SLAB_LANES:(s + 1) * S5_SLAB_LANES],
                     sim_ref[:, s * S5_SLAB_LANES:(s + 1) * S5_SLAB_LANES], cre_ref, cim_ref, d_ref, s)
         for s in range(S5_SLABS)], axis=1)
    ob = _glu(g, wa_ref, wb_ref)

    for sl in range(n_lane_slabs):
        os_ref[sl] = ob[:, sl * LANES:(sl + 1) * LANES]
    for b in range(BATCH):
        for sl in range(n_lane_slabs):
            ob_ref[b, :, sl * LANES:(sl + 1) * LANES] = os_ref[sl, pl.ds(b, tt, stride=BATCH), :]


def _s5_decode_kernel(u_ref, s0re_ref, s0im_ref, lre_ref, lim_ref, bre_ref, bim_ref, cre_ref, cim_ref, d_ref,
                      wa_ref, wb_ref, ob_ref, fre_ref, fim_ref):
    gs = []
    for s in range(S5_SLABS):
        cols = slice(s * S5_SLAB_LANES, (s + 1) * S5_SLAB_LANES)
        u = u_ref[:, s * MXU_DIM:(s + 1) * MXU_DIM]
        ub = u.astype(BF16)
        lre, lim = lre_ref[:, cols], lim_ref[:, cols]
        pre, pim = s0re_ref[:, cols], s0im_ref[:, cols]
        nre = lre * pre - lim * pim + jnp.dot(ub, bre_ref[s], preferred_element_type=F32)
        nim = lre * pim + lim * pre + jnp.dot(ub, bim_ref[s], preferred_element_type=F32)
        fre_ref[:, cols] = nre
        fim_ref[:, cols] = nim
        gs.append(_s5_readout(u, nre, nim, cre_ref, cim_ref, d_ref, s))
    ob_ref[...] = _glu(jnp.concatenate(gs, axis=1), wa_ref, wb_ref)


def _s5_weight_specs(layer, n_grid):
    def const(shape, *idx):
        return pl.BlockSpec(shape, lambda *_: idx, pipeline_mode=pl.Buffered(1))

    return [
        const((None, 1, SSM_LANES), layer, 0, 0),
        const((None, 1, SSM_LANES), layer, 0, 0),
        const((None, S5_SLABS, MXU_DIM, S5_SLAB_LANES), layer, 0, 0, 0),
        const((None, S5_SLABS, MXU_DIM, S5_SLAB_LANES), layer, 0, 0, 0),
        const((None, S5_SLABS, S5_SLAB_LANES, MXU_DIM), layer, 0, 0, 0),
        const((None, S5_SLABS, S5_SLAB_LANES, MXU_DIM), layer, 0, 0, 0),
        const((None, 1, D_MODEL), layer, 0, 0),
        const((None, D_MODEL, D_MODEL), layer, 0, 0),
        const((None, D_MODEL, D_MODEL), layer, 0, 0),
    ]


def _s5_prompt(proj3, layer, s5w):
    tt = 64
    rows = tt * BATCH
    kern = functools.partial(_s5_prompt_kernel, tt=tt)
    state = jax.ShapeDtypeStruct((BATCH, SSM_LANES), F32)
    state_spec = pl.BlockSpec((BATCH, SSM_LANES), lambda i: (0, 0))
    return pl.pallas_call(
        kern,
        out_shape=(jax.ShapeDtypeStruct((BATCH, SEQ, D_MODEL), F32), state, state),
        grid=(SEQ // tt,),
        in_specs=[pl.BlockSpec((BATCH, tt, D_MODEL), lambda i: (0, i, COL_U))] + _s5_weight_specs(layer, 1),
        out_specs=(pl.BlockSpec((BATCH, tt, D_MODEL), lambda i: (0, i, 0)), state_spec, state_spec),
        scratch_shapes=[
            pltpu.VMEM((D_MODEL // LANES, rows, LANES), F32),
            pltpu.VMEM((rows, SSM_LANES), F32),
            pltpu.VMEM((rows, SSM_LANES), F32),
            pltpu.VMEM((D_MODEL // LANES, rows, LANES), F32),
            pltpu.VMEM((2, BATCH, SSM_LANES), F32),
        ],
        compiler_params=_params("arbitrary"),
    )(proj3, *s5w)


def _s5_decode(proj, s0re, s0im, layer, s5w):
    m = proj.shape[0]
    state = jax.ShapeDtypeStruct((m, SSM_LANES), F32)
    full = pl.BlockSpec((m, SSM_LANES), lambda i: (0, 0))
    return pl.pallas_call(
        _s5_decode_kernel,
        out_shape=(jax.ShapeDtypeStruct((m, D_MODEL), F32), state, state),
        grid=(1,),
        in_specs=[pl.BlockSpec((m, D_MODEL), lambda i: (0, COL_U)),
                  pl.BlockSpec((None, m, SSM_LANES), lambda i: (layer, 0, 0)),
                  pl.BlockSpec((None, m, SSM_LANES), lambda i: (layer, 0, 0))] + _s5_weight_specs(layer, 1),
        out_specs=(pl.BlockSpec((m, D_MODEL), lambda i: (0, 0)), full, full),
        compiler_params=_params("arbitrary"),
    )(proj, s0re, s0im, *s5w)


def _topk_select(gate, valid, n_candidates):
    lane = lax.broadcasted_iota(jnp.int32, gate.shape, 1)
    g = jnp.where(valid, gate, NEG)
    rank = jnp.zeros(gate.shape, jnp.int32)
    for jp in range(n_candidates):
        col = g[:, jp:jp + 1]
        rank = rank + jnp.where(col > g, 1, jnp.where(col == g, jnp.where(lane > jp, 1, 0), 0))
    return jnp.where(valid, jnp.where(rank < MOBA_TOPK, 1.0, 0.0), 0.0)


def _moba_prompt_kernel(q_ref, k_ref, v_ref, *refs):
    o_ref, ko_ref, vo_ref = refs[-3:]
    nb = SEQ // MOBA_BLOCK
    k = k_ref[...]
    v = v_ref[...]
    head_rows = pl.ds(pl.program_id(1), SEQ, stride=N_KV_HEADS)
    ko_ref[head_rows, :] = k
    vo_ref[head_rows, :] = v
    kb = k.astype(BF16)
    vb = v.astype(BF16)
    kmean = jnp.mean(k.reshape(nb, MOBA_BLOCK, HEAD_DIM), axis=1).astype(BF16)
    rows = KV_GROUP * MOBA_BLOCK
    lane = lax.broadcasted_iota(jnp.int32, (rows, nb), 1)
    q_in_blk = lax.broadcasted_iota(jnp.int32, (rows, MOBA_BLOCK), 0) & (MOBA_BLOCK - 1)
    k_in_blk = lax.broadcasted_iota(jnp.int32, (rows, MOBA_BLOCK), 1)
    causal = k_in_blk <= q_in_blk
    nt = (((1,), (1,)), ((), ()))
    for i in range(nb):
        q2 = q_ref[i * MOBA_BLOCK:(i + 1) * MOBA_BLOCK, :]
        q = jnp.concatenate([q2[:, h * HEAD_DIM:(h + 1) * HEAD_DIM] for h in range(KV_GROUP)], axis=0)
        nk = (i + 1) * MOBA_BLOCK
        s = lax.dot_general((q * ATTN_SCALE).astype(BF16), kb[:nk], nt, preferred_element_type=F32)
        parts = []
        if i > 0:
            gate = lax.dot_general(q.astype(BF16), kmean, nt, preferred_element_type=F32)
            sel = _topk_select(gate, lane < i, i)
            for j in range(i):
                parts.append(jnp.where(sel[:, j:j + 1] > 0.0, s[:, j * MOBA_BLOCK:(j + 1) * MOBA_BLOCK], NEG))
        parts.append(jnp.where(causal, s[:, i * MOBA_BLOCK:], NEG))
        s = jnp.concatenate(parts, axis=1) if len(parts) > 1 else parts[0]
        m = jnp.max(s, axis=-1, keepdims=True)
        p = jnp.exp(s - m)
        l = jnp.sum(p, axis=-1, keepdims=True)
        o = jnp.dot(p.astype(BF16), vb[:nk], preferred_element_type=F32) / l
        for h in range(KV_GROUP):
            o_ref[i * MOBA_BLOCK:(i + 1) * MOBA_BLOCK, h * HEAD_DIM:(h + 1) * HEAD_DIM] = (
                o[h * MOBA_BLOCK:(h + 1) * MOBA_BLOCK])


def _moba_prompt(proj3, layer, kv_stacks):
    gw = KV_GROUP * HEAD_DIM
    kv = jax.ShapeDtypeStruct((DEPTH, BATCH, SEQ * N_KV_HEADS, HEAD_DIM), F32)
    kv_spec = pl.BlockSpec((None, None, SEQ * N_KV_HEADS, HEAD_DIM), lambda b, h: (layer, b, 0, 0))
    in_specs = [
        pl.BlockSpec((None, SEQ, gw), lambda b, h: (b, 0, COL_Q * D_MODEL // gw + h)),
        pl.BlockSpec((None, SEQ, HEAD_DIM), lambda b, h: (b, 0, COL_K // HEAD_DIM + h)),
        pl.BlockSpec((None, SEQ, HEAD_DIM), lambda b, h: (b, 0, COL_V // HEAD_DIM + h)),
    ]
    args = [proj3, proj3, proj3]
    aliases = {}
    if kv_stacks is not None:
        in_specs += [pl.BlockSpec(memory_space=pl.ANY)] * 2
        args += list(kv_stacks)
        aliases = {3: 1, 4: 2}
    out_c, k_stack, v_stack = pl.pallas_call(
        _moba_prompt_kernel,
        out_shape=(jax.ShapeDtypeStruct((BATCH, SEQ, D_MODEL), F32), kv, kv),
        grid=(BATCH, N_KV_HEADS),
        in_specs=in_specs,
        out_specs=(pl.BlockSpec((None, SEQ, gw), lambda b, h: (b, 0, h)), kv_spec, kv_spec),
        input_output_aliases=aliases,
        compiler_params=_params("parallel", "arbitrary"),
    )(*args)
    return out_c, (k_stack, v_stack)


def _moba_decode_kernel(pt_ref, q_ref, kn_ref, vn_ref, *refs, n_pages):
    del pt_ref
    assert SUBLANES == 2 * N_KV_HEADS
    k_pages = refs[:n_pages]
    v_pages = refs[n_pages:2 * n_pages]
    o_ref = refs[2 * n_pages]
    per = MOBA_BLOCK // PAGE_SIZE
    nb = n_pages // per
    nt = (((1,), (1,)), ((), ()))
    q = q_ref[...]
    qs = (q * ATTN_SCALE).astype(BF16)
    head_kv = lax.broadcasted_iota(jnp.int32, (N_HEADS, PAGE_ROWS), 0) >> KV_GROUP_SHIFT
    row_kv = lax.broadcasted_iota(jnp.int32, (N_HEADS, PAGE_ROWS), 1) & (N_KV_HEADS - 1)
    own_kv = row_kv == head_kv
    scores, ksums = [], []
    for pg in range(n_pages):
        kp = k_pages[pg][...]
        scores.append(lax.dot_general(qs, kp.astype(BF16), nt, preferred_element_type=F32))
        ksums.append(jnp.sum(kp.reshape(PAGE_ROWS // SUBLANES, SUBLANES, HEAD_DIM), axis=0))
    kmeans = []
    for j in range(nb):
        ks = ksums[per * j]
        for r in range(1, per):
            ks = ks + ksums[per * j + r]
        ks = ks + pltpu.roll(ks, N_KV_HEADS, 0)
        kmeans.append(ks * (1.0 / MOBA_BLOCK))
    kmean = jnp.concatenate(kmeans, axis=0).astype(BF16)
    gate = lax.dot_general(q.astype(BF16), kmean, nt, preferred_element_type=F32)
    n_cand = nb * SUBLANES
    cand = lax.broadcasted_iota(jnp.int32, (N_HEADS, n_cand), 1) & (SUBLANES - 1)
    cand_head_kv = lax.broadcasted_iota(jnp.int32, (N_HEADS, n_cand), 0) >> KV_GROUP_SHIFT
    sel = _topk_select(gate, cand == cand_head_kv, n_cand)
    masked = []
    for j in range(nb):
        on = jnp.max(sel[:, j * SUBLANES:(j + 1) * SUBLANES], axis=-1, keepdims=True) > 0.0
        for r in range(per):
            masked.append(jnp.where(own_kv, jnp.where(on, scores[per * j + r], NEG), NEG))
    kn = kn_ref[...]
    vn = vn_ref[...]
    kn_h = jnp.concatenate([kn[h // KV_GROUP:h // KV_GROUP + 1] for h in range(N_HEADS)], axis=0)
    vn_h = jnp.concatenate([vn[h // KV_GROUP:h // KV_GROUP + 1] for h in range(N_HEADS)], axis=0)
    s_own = jnp.sum(q * ATTN_SCALE * kn_h, axis=-1, keepdims=True)
    m = jnp.maximum(jnp.max(functools.reduce(jnp.maximum, masked), axis=-1, keepdims=True), s_own)
    p_own = jnp.exp(s_own - m)
    acc = p_own * vn_h
    p_sum = jnp.zeros((N_HEADS, PAGE_ROWS), F32)
    for pg in range(n_pages):
        p = jnp.exp(masked[pg] - m)
        p_sum = p_sum + p
        acc = acc + jnp.dot(p.astype(BF16), v_pages[pg][...].astype(BF16), preferred_element_type=F32)
    l = jnp.sum(p_sum, axis=-1, keepdims=True) + p_own
    o_ref[...] = acc / l


def _moba_decode(proj, cache_k, cache_v, page_table, layer):
    m = proj.shape[0]
    n_pages = page_table.shape[1]
    q = proj[:, COL_Q * D_MODEL:(COL_Q + 1) * D_MODEL].reshape(m, N_HEADS, HEAD_DIM)
    kn = proj[:, COL_K:COL_K + KV_DIM].reshape(m, N_KV_HEADS, HEAD_DIM)
    vn = proj[:, COL_V:COL_V + KV_DIM].reshape(m, N_KV_HEADS, HEAD_DIM)

    def page_spec(pg):
        return pl.BlockSpec((None, None, PAGE_ROWS, HEAD_DIM), lambda b, pt: (layer, pt[b, pg], 0, 0))

    grid_spec = pltpu.PrefetchScalarGridSpec(
        num_scalar_prefetch=1,
        grid=(m,),
        in_specs=[
            pl.BlockSpec((None, N_HEADS, HEAD_DIM), lambda b, pt: (b, 0, 0)),
            pl.BlockSpec((None, N_KV_HEADS, HEAD_DIM), lambda b, pt: (b, 0, 0)),
            pl.BlockSpec((None, N_KV_HEADS, HEAD_DIM), lambda b, pt: (b, 0, 0)),
        ] + [page_spec(pg) for pg in range(n_pages)] + [page_spec(pg) for pg in range(n_pages)],
        out_specs=pl.BlockSpec((None, N_HEADS, HEAD_DIM), lambda b, pt: (b, 0, 0)),
    )
    out = pl.pallas_call(
        functools.partial(_moba_decode_kernel, n_pages=n_pages),
        out_shape=jax.ShapeDtypeStruct((m, N_HEADS, HEAD_DIM), F32),
        grid_spec=grid_spec,
        compiler_params=_params("parallel"),
    )(page_table, q, kn, vn, *([cache_k] * n_pages), *([cache_v] * n_pages))
    return out.reshape(m, D_MODEL)


def _merge_tail(x, gate, cb, zc, ga, gb, gc, ob, oc, wc_ref, wo_ref):
    out_a = jnp.dot((cb * zc).astype(BF16), wc_ref[...], preferred_element_type=F32)
    merged = jax.nn.sigmoid(ga) * out_a + jax.nn.sigmoid(gb) * ob + jax.nn.sigmoid(gc) * oc
    return x + gate * jnp.dot(merged.astype(BF16), wo_ref[...], preferred_element_type=F32)


def _mix_prompt_kernel(x_ref, gate_ref, cb_ref, cc_ref, cx_ref, ga_ref, gb_ref, gc_ref, ob_ref, oc_ref,
                       cw_ref, wc_ref, wo_ref, xo_ref, last_ref, z_ref, *, tm, tiles_per_seq):
    tail = SUBLANES

    @pl.when(pl.program_id(0) % tiles_per_seq == 0)
    def _():
        z_ref[0:tail, :] = jnp.zeros((tail, D_MODEL), F32)

    z = cc_ref[...] * cx_ref[...]
    z_ref[tail:tail + tm, :] = z
    w = cw_ref[...]
    zc = (z_ref[tail - 2:tail - 2 + tm, :] * w[0:1] + z_ref[tail - 1:tail - 1 + tm, :] * w[1:2] + z * w[2:3])
    z_last = z[tm - tail:, :]
    z_ref[0:tail, :] = z_last
    last_ref[...] = z_last
    xo_ref[...] = _merge_tail(x_ref[...], gate_ref[...], cb_ref[...], zc, ga_ref[...], gb_ref[...], gc_ref[...],
                              ob_ref[...], oc_ref[...], wc_ref, wo_ref)


def _mix_decode_kernel(x_ref, gate_ref, cb_ref, cc_ref, cx_ref, ga_ref, gb_ref, gc_ref, ob_ref, oc_ref,
                       cw_ref, wc_ref, wo_ref, p0_ref, p1_ref, xo_ref, last_ref):
    z = cc_ref[...] * cx_ref[...]
    w = cw_ref[...]
    p1 = p1_ref[...]
    zc = p0_ref[...] * w[0:1] + p1 * w[1:2] + z * w[2:3]
    last_ref[:, 0:D_MODEL] = p1
    last_ref[:, D_MODEL:2 * D_MODEL] = z
    xo_ref[...] = _merge_tail(x_ref[...], gate_ref[...], cb_ref[...], zc, ga_ref[...], gb_ref[...], gc_ref[...],
                              ob_ref[...], oc_ref[...], wc_ref, wo_ref)


def _mix(x, mods, layer, proj, out_b, out_c, conv_w, w_conv_out, w_o, decode, conv_state=None):
    m = x.shape[0]
    tm = m if decode else 256
    gate_arr, gate_spec = _mod_spec(*mods, layer, 2, tm, SEQ, decode)

    def col(c):
        return pl.BlockSpec((tm, D_MODEL), lambda i: (i, c))

    row = pl.BlockSpec((tm, D_MODEL), lambda i: (i, 0))
    wspec = pl.BlockSpec((None, D_MODEL, D_MODEL), lambda i: (layer, 0, 0))
    in_specs = [row, gate_spec, col(COL_CB), col(COL_CC), col(COL_CX), col(COL_GA), col(COL_GB), col(COL_GC),
                row, row, pl.BlockSpec((None, CONV_WIDTH, D_MODEL), lambda i: (layer, 0, 0)), wspec, wspec]
    args = [x, gate_arr, proj, proj, proj, proj, proj, proj, out_b, out_c, conv_w, w_conv_out, w_o]
    if decode:
        state2 = conv_state.reshape(DEPTH, m, (CONV_WIDTH - 1) * D_MODEL)
        in_specs += [pl.BlockSpec((None, m, D_MODEL), lambda i: (layer, 0, 0)),
                     pl.BlockSpec((None, m, D_MODEL), lambda i: (layer, 0, 1))]
        args += [state2, state2]
        return pl.pallas_call(
            _mix_decode_kernel,
            out_shape=(jax.ShapeDtypeStruct((m, D_MODEL), F32),
                       jax.ShapeDtypeStruct((m, (CONV_WIDTH - 1) * D_MODEL), F32)),
            grid=(1,),
            in_specs=in_specs,
            out_specs=(row, pl.BlockSpec((m, (CONV_WIDTH - 1) * D_MODEL), lambda i: (0, 0))),
            compiler_params=_params("arbitrary"),
        )(*args)
    tiles_per_seq = SEQ // tm
    return pl.pallas_call(
        functools.partial(_mix_prompt_kernel, tm=tm, tiles_per_seq=tiles_per_seq),
        out_shape=(jax.ShapeDtypeStruct((m, D_MODEL), F32),
                   jax.ShapeDtypeStruct((BATCH, SUBLANES, D_MODEL), F32)),
        grid=(m // tm,),
        in_specs=in_specs,
        out_specs=(row, pl.BlockSpec((None, SUBLANES, D_MODEL), lambda i: (i // tiles_per_seq, 0, 0))),
        scratch_shapes=[pltpu.VMEM((SUBLANES + tm, D_MODEL), F32)],
        compiler_params=_params("arbitrary"),
    )(*args)


def _ffn_kernel(x_ref, g_ref, sc_ref, sh_ref, gate_ref, w1_ref, w3_ref, w2_ref, o_ref, h_ref, acc_ref):
    f = pl.program_id(1)

    @pl.when(f == 0)
    def _():
        h_ref[...] = _mod_norm(x_ref[...], g_ref[...], sc_ref[...], sh_ref[...]).astype(BF16)
        acc_ref[...] = jnp.zeros_like(acc_ref)

    h = h_ref[...]
    a = (jax.nn.silu(jnp.dot(h, w1_ref[...], preferred_element_type=F32))
         * jnp.dot(h, w3_ref[...], preferred_element_type=F32))
    acc_ref[...] += jnp.dot(a.astype(BF16), w2_ref[...], preferred_element_type=F32)

    @pl.when(f == pl.num_programs(1) - 1)
    def _():
        o_ref[...] = x_ref[...] + gate_ref[...] * acc_ref[...]


def _ffn(x, mods, layer, g_ffn, w1, w3, w2, decode):
    m = x.shape[0]
    tm = m if decode else 512
    fc = D_FF // 2
    sc_arr, sc_spec = _mod_spec(*mods, layer, 4, tm, SEQ, decode)
    sh_arr, sh_spec = _mod_spec(*mods, layer, 3, tm, SEQ, decode)
    gt_arr, gt_spec = _mod_spec(*mods, layer, 5, tm, SEQ, decode)
    row = pl.BlockSpec((tm, D_MODEL), lambda i, f: (i, 0))
    return pl.pallas_call(
        _ffn_kernel,
        out_shape=jax.ShapeDtypeStruct((m, D_MODEL), F32),
        grid=(m // tm, D_FF // fc),
        in_specs=[row, pl.BlockSpec((None, 1, D_MODEL), lambda i, f: (layer, 0, 0)), sc_spec, sh_spec, gt_spec,
                  pl.BlockSpec((None, D_MODEL, fc), lambda i, f: (layer, 0, f)),
                  pl.BlockSpec((None, D_MODEL, fc), lambda i, f: (layer, 0, f)),
                  pl.BlockSpec((None, fc, D_MODEL), lambda i, f: (layer, f, 0))],
        out_specs=row,
        scratch_shapes=[pltpu.VMEM((tm, D_MODEL), BF16), pltpu.VMEM((tm, D_MODEL), F32)],
        compiler_params=_params("parallel", "arbitrary"),
    )(x, g_ffn, sc_arr, sh_arr, gt_arr, w1, w3, w2)


def _final_norm_kernel(x_ref, g_ref, o_ref):
    x = x_ref[...]
    o_ref[...] = x * lax.rsqrt(jnp.mean(x * x, axis=-1, keepdims=True) + EPS) * g_ref[...]


def _final_norm(x, g):
    m = x.shape[0]
    tm = min(m, 1024)
    row = pl.BlockSpec((tm, D_MODEL), lambda i: (i, 0))
    return pl.pallas_call(
        _final_norm_kernel,
        out_shape=jax.ShapeDtypeStruct((m, D_MODEL), F32),
        grid=(m // tm,),
        in_specs=[row, pl.BlockSpec((1, D_MODEL), lambda i: (0, 0))],
        out_specs=row,
        compiler_params=_params("parallel"),
    )(x, g.reshape(1, D_MODEL))


def _trunk(x, mods, w, decode, state=None):
    ks, vs, convs, res, ims = [], [], [], [], []
    kv_stacks = None
    for l in range(DEPTH):
        proj = _in_proj(x, mods, l, w["g_mix"], w["w_in"], decode)
        if decode:
            out_b, s_re, s_im = _s5_decode(proj, state["ssm_re"], state["ssm_im"], l, w["s5"])
            out_c = _moba_decode(proj, state["cache_k"], state["cache_v"], state["page_table"], l)
            ks.append(proj[:, COL_K:COL_K + KV_DIM])
            vs.append(proj[:, COL_V:COL_V + KV_DIM])
            x, conv_last = _mix(x, mods, l, proj, out_b, out_c, w["conv_w"], w["w_conv_out"], w["w_o"], True,
                                state["conv"])
            conv_last = conv_last.reshape(DEC_BATCH, CONV_WIDTH - 1, D_MODEL)
        else:
            proj3 = proj.reshape(BATCH, SEQ, D_IN)
            out_b, s_re, s_im = _s5_prompt(proj3, l, w["s5"])
            out_c, kv_stacks = _moba_prompt(proj3, l, kv_stacks)
            x, conv_last = _mix(x, mods, l, proj, out_b.reshape(BATCH * SEQ, D_MODEL),
                                out_c.reshape(BATCH * SEQ, D_MODEL), w["conv_w"], w["w_conv_out"], w["w_o"], False)
            conv_last = conv_last[:, SUBLANES - (CONV_WIDTH - 1):, :]
        x = _ffn(x, mods, l, w["g_ffn"], w["w_ff1"], w["w_ff3"], w["w_ff2"], decode)
        convs.append(conv_last)
        res.append(s_re)
        ims.append(s_im)
    y = _final_norm(x, w["g_final"])
    k_all, v_all = (jnp.stack(ks), jnp.stack(vs)) if decode else kv_stacks
    return y, k_all, v_all, jnp.stack(convs), jnp.stack(res), jnp.stack(ims)


def kernel(x_prompt, x_sample, cache_k, cache_v, state_conv, state_ssm_re, state_ssm_im, page_table, c_prompt, c_sample, w_ada, b_ada, g_mix, w_in, conv_w, w_conv_out, ssm_lam_re, ssm_lam_im, ssm_log_step, ssm_b_re, ssm_b_im, ssm_c_re, ssm_c_im, ssm_d, w_glu_a, w_glu_b, w_o, g_ffn, w_ff1, w_ff3, w_ff2, g_final):
    n_pool = cache_k.shape[1]
    mod = _ada(jnp.concatenate([c_sample, c_prompt], axis=0), w_ada, b_ada)
    mods = (mod, mod.reshape(DEPTH, DEC_BATCH + BATCH, 6, 1, D_MODEL))

    lam_re, lam_im, bb_re, bb_im = _s5_discretize(
        ssm_lam_re, ssm_lam_im, ssm_log_step, jnp.swapaxes(ssm_b_re, 2, 3), jnp.swapaxes(ssm_b_im, 2, 3))
    s5w = (lam_re.reshape(DEPTH, 1, SSM_LANES), lam_im.reshape(DEPTH, 1, SSM_LANES),
           _block_diag_in(bb_re).astype(BF16), _block_diag_in(bb_im).astype(BF16),
           _block_diag_out(ssm_c_re).astype(BF16), _block_diag_out(ssm_c_im).astype(BF16),
           ssm_d.reshape(DEPTH, 1, D_MODEL), w_glu_a.astype(BF16), w_glu_b.astype(BF16))
    w = dict(g_mix=g_mix.reshape(DEPTH, 1, D_MODEL), w_in=w_in.astype(BF16), s5=s5w, conv_w=conv_w,
             w_conv_out=w_conv_out.astype(BF16), w_o=w_o.astype(BF16), g_ffn=g_ffn.reshape(DEPTH, 1, D_MODEL),
             w_ff1=w_ff1.astype(BF16), w_ff3=w_ff3.astype(BF16), w_ff2=w_ff2.astype(BF16), g_final=g_final)

    yp, kp, vp, cp, rp, ip = _trunk(x_prompt.reshape(BATCH * SEQ, D_MODEL), mods, w, False)
    state = dict(ssm_re=state_ssm_re.reshape(DEPTH, DEC_BATCH, SSM_LANES),
                 ssm_im=state_ssm_im.reshape(DEPTH, DEC_BATCH, SSM_LANES),
                 cache_k=cache_k.reshape(DEPTH, n_pool, PAGE_ROWS, HEAD_DIM),
                 cache_v=cache_v.reshape(DEPTH, n_pool, PAGE_ROWS, HEAD_DIM),
                 page_table=page_table, conv=state_conv)
    ys, ksm, vsm, cs, rs, is_ = _trunk(x_sample.reshape(DEC_BATCH, D_MODEL), mods, w, True, state)

    kv_p = (DEPTH, BATCH, SEQ, N_KV_HEADS, HEAD_DIM)
    kv_s = (DEPTH, DEC_BATCH, 1, N_KV_HEADS, HEAD_DIM)
    st_p = (DEPTH, BATCH, SSM_GROUPS, SSM_STATE)
    st_s = (DEPTH, DEC_BATCH, SSM_GROUPS, SSM_STATE)
    return (yp.reshape(BATCH, SEQ, D_MODEL), ys.reshape(DEC_BATCH, 1, D_MODEL),
            kp.reshape(kv_p), vp.reshape(kv_p), cp, rp.reshape(st_p), ip.reshape(st_p),
            ksm.reshape(kv_s), vsm.reshape(kv_s), cs, rs.reshape(st_s), is_.reshape(st_s))
```

```python
import functools

import jax
import jax.numpy as jnp
from jax import lax
from jax.experimental import pallas as pl
from jax.experimental.pallas import tpu as pltpu

F32 = jnp.float32
BF16 = jnp.bfloat16

D_MODEL = 1024
DEPTH = 4
SEQ = 2048
BATCH = 8
DEC_BATCH = 128
PAGE_SIZE = 128
EPS = 1e-6
CONV_WIDTH = 3
SSM_GROUP = 16
SSM_GROUPS = D_MODEL // SSM_GROUP
SSM_STATE = 64
SSM_LANES = SSM_GROUPS * SSM_STATE
HEAD_DIM = 128
N_HEADS = D_MODEL // HEAD_DIM
N_KV_HEADS = N_HEADS // 2
KV_GROUP = N_HEADS // N_KV_HEADS
KV_DIM = N_KV_HEADS * HEAD_DIM
KV_GROUP_SHIFT = KV_GROUP.bit_length() - 1
assert KV_GROUP == 1 << KV_GROUP_SHIFT and N_KV_HEADS & (N_KV_HEADS - 1) == 0
PAGE_ROWS = PAGE_SIZE * N_KV_HEADS
MOBA_BLOCK = 256
MOBA_TOPK = 3
ATTN_SCALE = HEAD_DIM ** -0.5
NEG = -1e30
D_FF = 2816
D_IN = 9216
COL_CB, COL_CC, COL_CX, COL_U, COL_GA, COL_GB, COL_GC = 0, 1, 2, 3, 4, 5, 6
D_MIX = 7 * D_MODEL
COL_Q = 0
COL_K = D_MODEL
COL_V = COL_K + KV_DIM
D_QKV = D_MODEL + 2 * KV_DIM
W_QKV = 4 * D_MODEL

LANES = 128
SUBLANES = 8
MXU_DIM = 256
VMEM_LIMIT = 56 * 1024 * 1024

S5_SLABS = D_MODEL // MXU_DIM
S5_SLAB_LANES = SSM_LANES // S5_SLABS


def _params(*sem):
    return pltpu.CompilerParams(dimension_semantics=sem, vmem_limit_bytes=VMEM_LIMIT)


def _ada_kernel(c_ref, w_ref, b_ref, o_ref):
    c = jax.nn.silu(c_ref[...]).astype(BF16)
    o_ref[...] = jnp.dot(c, w_ref[...].astype(BF16), preferred_element_type=F32) + b_ref[...]


def _ada(c_all, w_ada, b_ada):
    n_rows = c_all.shape[0]
    tn = 1536
    return pl.pallas_call(
        _ada_kernel,
        out_shape=jax.ShapeDtypeStruct((DEPTH, n_rows, 6 * D_MODEL), F32),
        grid=(DEPTH, 6 * D_MODEL // tn),
        in_specs=[
            pl.BlockSpec((n_rows, D_MODEL), lambda l, j: (0, 0)),
            pl.BlockSpec((None, D_MODEL, tn), lambda l, j: (l, 0, j)),
            pl.BlockSpec((None, 1, tn), lambda l, j: (l, 0, j)),
        ],
        out_specs=pl.BlockSpec((None, n_rows, tn), lambda l, j: (l, 0, j)),
        compiler_params=_params("parallel", "parallel"),
    )(c_all, w_ada, b_ada.reshape(DEPTH, 1, 6 * D_MODEL))


def _s5_disc_kernel(lre_ref, lim_ref, ls_ref, bre_ref, bim_ref, lbre_ref, lbim_ref, bbre_ref, bbim_ref):
    lre = lre_ref[...]
    lim = lim_ref[...]
    step = jnp.exp(ls_ref[...])
    mag = jnp.exp(lre * step)
    ang = lim * step
    lbre = mag * jnp.cos(ang)
    lbim = mag * jnp.sin(ang)
    lbre_ref[...] = lbre
    lbim_ref[...] = lbim
    nre = lbre - 1.0
    den = lre * lre + lim * lim
    cre = (nre * lre + lbim * lim) / den
    cim = (lbim * lre - nre * lim) / den
    bre = bre_ref[...]
    bim = bim_ref[...]
    bbre_ref[...] = cre * bre - cim * bim
    bbim_ref[...] = cre * bim + cim * bre


def _s5_discretize(lam_re, lam_im, log_step, b_re_t, b_im_t):
    g, p, i = SSM_GROUPS, SSM_STATE, SSM_GROUP
    gp = pl.BlockSpec((None, g, 1, p), lambda l: (l, 0, 0, 0))
    gip = pl.BlockSpec((None, g, i, p), lambda l: (l, 0, 0, 0))
    return pl.pallas_call(
        _s5_disc_kernel,
        out_shape=(jax.ShapeDtypeStruct((DEPTH, g, 1, p), F32), jax.ShapeDtypeStruct((DEPTH, g, 1, p), F32),
                   jax.ShapeDtypeStruct((DEPTH, g, i, p), F32), jax.ShapeDtypeStruct((DEPTH, g, i, p), F32)),
        grid=(DEPTH,),
        in_specs=[gp, gp, pl.BlockSpec((None, g, 1, 1), lambda l: (l, 0, 0, 0)), gip, gip],
        out_specs=(gp, gp, gip, gip),
        compiler_params=_params("parallel"),
    )(lam_re.reshape(DEPTH, g, 1, p), lam_im.reshape(DEPTH, g, 1, p), log_step.reshape(DEPTH, g, 1, 1),
      b_re_t, b_im_t)


def _block_diag_in(b_t):
    gs = SSM_GROUPS // S5_SLABS
    b = b_t.reshape(DEPTH, S5_SLABS, gs, SSM_GROUP, SSM_STATE)
    eye = jnp.eye(gs, dtype=b_t.dtype)
    out = b[:, :, :, :, None, :] * eye[None, None, :, None, :, None]
    return out.reshape(DEPTH, S5_SLABS, gs * SSM_GROUP, gs * SSM_STATE)


def _block_diag_out(c):
    gs = SSM_GROUPS // S5_SLABS
    ct = jnp.swapaxes(c, 2, 3).reshape(DEPTH, S5_SLABS, gs, SSM_STATE, SSM_GROUP)
    eye = jnp.eye(gs, dtype=c.dtype)
    out = ct[:, :, :, :, None, :] * eye[None, None, :, None, :, None]
    return out.reshape(DEPTH, S5_SLABS, gs * SSM_STATE, gs * SSM_GROUP)


def _mod_norm(x, g, scale, shift):
    y = x * lax.rsqrt(jnp.mean(x * x, axis=-1, keepdims=True) + EPS)
    return (y * g) * (1.0 + scale) + shift


def _in_proj_kernel(x_ref, g_ref, sc_ref, sh_ref, w_ref, o_ref, h_ref):
    @pl.when(pl.program_id(1) == 0)
    def _():
        h_ref[...] = _mod_norm(x_ref[...], g_ref[...], sc_ref[...], sh_ref[...]).astype(BF16)

    o_ref[...] = jnp.dot(h_ref[...], w_ref[...], preferred_element_type=F32).astype(o_ref.dtype)


def _mod_spec(mod, mod5, layer, chunk, tm, rows_per_seq, decode):
    if decode:
        return mod, pl.BlockSpec((None, tm, D_MODEL), lambda i, *_: (layer, 0, chunk))
    return mod5, pl.BlockSpec((None, None, None, 1, D_MODEL),
                              lambda i, *_: (layer, DEC_BATCH + (i * tm) // rows_per_seq, chunk, 0, 0))


def _in_proj(x, mods, layer, g_mix, w_in, decode, qkv):
    m = x.shape[0]
    tm = m if decode else 2048
    tn = 512
    if qkv:
        n_out, dtype = D_QKV, F32
        w_tile = lambda j: W_QKV // tn + j
    else:
        n_out, dtype = D_MIX, BF16
        w_tile = lambda j: j + jnp.where(j >= W_QKV // tn, D_QKV // tn, 0)
    sc_arr, sc_spec = _mod_spec(*mods, layer, 1, tm, SEQ, decode)
    sh_arr, sh_spec = _mod_spec(*mods, layer, 0, tm, SEQ, decode)
    return pl.pallas_call(
        _in_proj_kernel,
        out_shape=jax.ShapeDtypeStruct((m, n_out), dtype),
        grid=(m // tm, n_out // tn),
        in_specs=[
            pl.BlockSpec((tm, D_MODEL), lambda i, j: (i, 0)),
            pl.BlockSpec((None, 1, D_MODEL), lambda i, j: (layer, 0, 0)),
            sc_spec, sh_spec,
            pl.BlockSpec((None, D_MODEL, tn), lambda i, j: (layer, 0, w_tile(j))),
        ],
        out_specs=pl.BlockSpec((tm, tn), lambda i, j: (i, j)),
        scratch_shapes=[pltpu.VMEM((tm, D_MODEL), BF16)],
        compiler_params=_params("parallel", "arbitrary"),
    )(x, g_mix, sc_arr, sh_arr, w_in)


def _s5_readout(u_slab, sre_slab, sim_slab, cre_ref, cim_ref, d_ref, s):
    y = (jnp.dot(sre_slab.astype(BF16), cre_ref[s], preferred_element_type=F32)
         - jnp.dot(sim_slab.astype(BF16), cim_ref[s], preferred_element_type=F32))
    y = y + u_slab * d_ref[:, s * MXU_DIM:(s + 1) * MXU_DIM]
    return jax.nn.gelu(y, approximate=True)


def _glu(g, wa_ref, wb_ref):
    gb = g.astype(BF16)
    return (jnp.dot(gb, wa_ref[...], preferred_element_type=F32)
            * jax.nn.sigmoid(jnp.dot(gb, wb_ref[...], preferred_element_type=F32)))


SCAN_LANES = 512


def _s5_prompt_kernel(u_ref, lre_ref, lim_ref, bre_ref, bim_ref, cre_ref, cim_ref, d_ref, wa_ref, wb_ref,
                      ob_ref, fre_ref, fim_ref,
                      us_ref, sre_ref, sim_ref, os_ref, st_ref, *, tt):
    n_lane_slabs = D_MODEL // LANES

    @pl.when(pl.program_id(0) == 0)
    def _():
        st_ref[...] = jnp.zeros_like(st_ref)

    for b in range(BATCH):
        for sl in range(n_lane_slabs):
            us_ref[sl, pl.ds(b, tt, stride=BATCH), :] = u_ref[b, :, sl * LANES:(sl + 1) * LANES].astype(F32)

    def u_slab(s):
        per = MXU_DIM // LANES
        return jnp.concatenate([us_ref[per * s + k] for k in range(per)], axis=1)

    gs = []
    for s in range(S5_SLABS):
        ub = u_slab(s).astype(BF16)
        slab = slice(s * S5_SLAB_LANES, (s + 1) * S5_SLAB_LANES)
        sre_ref[:, slab] = jnp.dot(ub, bre_ref[s], preferred_element_type=F32)
        sim_ref[:, slab] = jnp.dot(ub, bim_ref[s], preferred_element_type=F32)
        for c in range(S5_SLAB_LANES // SCAN_LANES):
            cols = slice(s * S5_SLAB_LANES + c * SCAN_LANES, s * S5_SLAB_LANES + (c + 1) * SCAN_LANES)
            lre = jnp.broadcast_to(lre_ref[:, cols], (BATCH, SCAN_LANES))
            lim = jnp.broadcast_to(lim_ref[:, cols], (BATCH, SCAN_LANES))
            pre, pim = st_ref[0, :, cols], st_ref[1, :, cols]
            for t in range(tt):
                rows = slice(t * BATCH, (t + 1) * BATCH)
                pre, pim = (lre * pre - lim * pim + sre_ref[rows, cols],
                            lre * pim + lim * pre + sim_ref[rows, cols])
                sre_ref[rows, cols] = pre
                sim_ref[rows, cols] = pim
            st_ref[0, :, cols] = pre
            st_ref[1, :, cols] = pim
        gs.append(_s5_readout(u_slab(s), sre_ref[:, slab], sim_ref[:, slab], cre_ref, cim_ref, d_ref, s))

    fre_ref[...] = st_ref[0]
    fim_ref[...] = st_ref[1]

    ob = _glu(jnp.concatenate(gs, axis=1), wa_ref, wb_ref)

    for sl in range(n_lane_slabs):
        os_ref[sl] = ob[:, sl * LANES:(sl + 1) * LANES]
    for b in range(BATCH):
        for sl in range(n_lane_slabs):
            ob_ref[b, :, sl * LANES:(sl + 1) * LANES] = os_ref[sl, pl.ds(b, tt, stride=BATCH), :]


def _s5_decode_kernel(u_ref, s0re_ref, s0im_ref, lre_ref, lim_ref, bre_ref, bim_ref, cre_ref, cim_ref, d_ref,
                      wa_ref, wb_ref, ob_ref, fre_ref, fim_ref):
    gs = []
    for s in range(S5_SLABS):
        cols = slice(s * S5_SLAB_LANES, (s + 1) * S5_SLAB_LANES)
        ub = u_ref[:, s * MXU_DIM:(s + 1) * MXU_DIM]
        u = ub.astype(F32)
        lre, lim = lre_ref[:, cols], lim_ref[:, cols]
        pre, pim = s0re_ref[:, cols], s0im_ref[:, cols]
        nre = lre * pre - lim * pim + jnp.dot(ub, bre_ref[s], preferred_element_type=F32)
        nim = lre * pim + lim * pre + jnp.dot(ub, bim_ref[s], preferred_element_type=F32)
        fre_ref[:, cols] = nre
        fim_ref[:, cols] = nim
        gs.append(_s5_readout(u, nre, nim, cre_ref, cim_ref, d_ref, s))
    ob_ref[...] = _glu(jnp.concatenate(gs, axis=1), wa_ref, wb_ref)


def _s5_weight_specs(layer, n_grid):
    def const(shape, *idx):
        return pl.BlockSpec(shape, lambda *_: idx, pipeline_mode=pl.Buffered(1))

    return [
        const((None, 1, SSM_LANES), layer, 0, 0),
        const((None, 1, SSM_LANES), layer, 0, 0),
        const((None, S5_SLABS, MXU_DIM, S5_SLAB_LANES), layer, 0, 0, 0),
        const((None, S5_SLABS, MXU_DIM, S5_SLAB_LANES), layer, 0, 0, 0),
        const((None, S5_SLABS, S5_SLAB_LANES, MXU_DIM), layer, 0, 0, 0),
        const((None, S5_SLABS, S5_SLAB_LANES, MXU_DIM), layer, 0, 0, 0),
        const((None, 1, D_MODEL), layer, 0, 0),
        const((None, D_MODEL, D_MODEL), layer, 0, 0),
        const((None, D_MODEL, D_MODEL), layer, 0, 0),
    ]


def _s5_prompt(mix3, layer, s5w):
    tt = 64
    rows = tt * BATCH
    kern = functools.partial(_s5_prompt_kernel, tt=tt)
    state = jax.ShapeDtypeStruct((BATCH, SSM_LANES), F32)
    state_spec = pl.BlockSpec((BATCH, SSM_LANES), lambda i: (0, 0))
    return pl.pallas_call(
        kern,
        out_shape=(jax.ShapeDtypeStruct((BATCH, SEQ, D_MODEL), F32), state, state),
        grid=(SEQ // tt,),
        in_specs=[pl.BlockSpec((BATCH, tt, D_MODEL), lambda i: (0, i, COL_U))] + _s5_weight_specs(layer, 1),
        out_specs=(pl.BlockSpec((BATCH, tt, D_MODEL), lambda i: (0, i, 0)), state_spec, state_spec),
        scratch_shapes=[
            pltpu.VMEM((D_MODEL // LANES, rows, LANES), F32),
            pltpu.VMEM((rows, SSM_LANES), F32),
            pltpu.VMEM((rows, SSM_LANES), F32),
            pltpu.VMEM((D_MODEL // LANES, rows, LANES), F32),
            pltpu.VMEM((2, BATCH, SSM_LANES), F32),
        ],
        compiler_params=_params("arbitrary"),
    )(mix3, *s5w)


def _s5_decode(mix, s0re, s0im, layer, s5w):
    m = mix.shape[0]
    state = jax.ShapeDtypeStruct((m, SSM_LANES), F32)
    full = pl.BlockSpec((m, SSM_LANES), lambda i: (0, 0))
    return pl.pallas_call(
        _s5_decode_kernel,
        out_shape=(jax.ShapeDtypeStruct((m, D_MODEL), F32), state, state),
        grid=(1,),
        in_specs=[pl.BlockSpec((m, D_MODEL), lambda i: (0, COL_U)),
                  pl.BlockSpec((None, m, SSM_LANES), lambda i: (layer, 0, 0)),
                  pl.BlockSpec((None, m, SSM_LANES), lambda i: (layer, 0, 0))] + _s5_weight_specs(layer, 1),
        out_specs=(pl.BlockSpec((m, D_MODEL), lambda i: (0, 0)), full, full),
        compiler_params=_params("arbitrary"),
    )(mix, s0re, s0im, *s5w)


def _topk_select(gate, valid, n_candidates):
    lane = lax.broadcasted_iota(jnp.int32, gate.shape, 1)
    g = jnp.where(valid, gate, NEG)
    rank = jnp.zeros(gate.shape, jnp.int32)
    for jp in range(n_candidates):
        col = g[:, jp:jp + 1]
        rank = rank + jnp.where(col > g, 1, jnp.where(col == g, jnp.where(lane > jp, 1, 0), 0))
    return jnp.where(valid, jnp.where(rank < MOBA_TOPK, 1.0, 0.0), 0.0)


def _moba_prompt_kernel(q_ref, k_ref, v_ref, *refs):
    o_ref, ko_ref, vo_ref = refs[-3:]
    nb = SEQ // MOBA_BLOCK
    k = k_ref[...]
    v = v_ref[...]
    head_rows = pl.ds(pl.program_id(1), SEQ, stride=N_KV_HEADS)
    ko_ref[head_rows, :] = k
    vo_ref[head_rows, :] = v
    kb = k.astype(BF16)
    vb = v.astype(BF16)
    kmean = jnp.mean(k.reshape(nb, MOBA_BLOCK, HEAD_DIM), axis=1).astype(BF16)
    rows = KV_GROUP * MOBA_BLOCK
    lane = lax.broadcasted_iota(jnp.int32, (rows, nb), 1)
    q_in_blk = lax.broadcasted_iota(jnp.int32, (rows, MOBA_BLOCK), 0) & (MOBA_BLOCK - 1)
    k_in_blk = lax.broadcasted_iota(jnp.int32, (rows, MOBA_BLOCK), 1)
    causal = k_in_blk <= q_in_blk
    nt = (((1,), (1,)), ((), ()))
    for i in range(nb):
        q2 = q_ref[i * MOBA_BLOCK:(i + 1) * MOBA_BLOCK, :]
        q = jnp.concatenate([q2[:, h * HEAD_DIM:(h + 1) * HEAD_DIM] for h in range(KV_GROUP)], axis=0)
        nk = (i + 1) * MOBA_BLOCK
        s = lax.dot_general((q * ATTN_SCALE).astype(BF16), kb[:nk], nt, preferred_element_type=F32)
        parts = []
        if i > 0:
            gate = lax.dot_general(q.astype(BF16), kmean, nt, preferred_element_type=F32)
            sel = _topk_select(gate, lane < i, i)
            for j in range(i):
                parts.append(jnp.where(sel[:, j:j + 1] > 0.0, s[:, j * MOBA_BLOCK:(j + 1) * MOBA_BLOCK], NEG))
        parts.append(jnp.where(causal, s[:, i * MOBA_BLOCK:], NEG))
        s = jnp.concatenate(parts, axis=1) if len(parts) > 1 else parts[0]
        m = jnp.max(s, axis=-1, keepdims=True)
        p = jnp.exp(s - m)
        l = jnp.sum(p, axis=-1, keepdims=True)
        o = jnp.dot(p.astype(BF16), vb[:nk], preferred_element_type=F32) / l
        for h in range(KV_GROUP):
            o_ref[i * MOBA_BLOCK:(i + 1) * MOBA_BLOCK, h * HEAD_DIM:(h + 1) * HEAD_DIM] = (
                o[h * MOBA_BLOCK:(h + 1) * MOBA_BLOCK])


def _moba_prompt(proj3, layer, kv_stacks):
    gw = KV_GROUP * HEAD_DIM
    kv = jax.ShapeDtypeStruct((DEPTH, BATCH, SEQ * N_KV_HEADS, HEAD_DIM), F32)
    kv_spec = pl.BlockSpec((None, None, SEQ * N_KV_HEADS, HEAD_DIM), lambda b, h: (layer, b, 0, 0))
    in_specs = [
        pl.BlockSpec((None, SEQ, gw), lambda b, h: (b, 0, COL_Q // gw + h)),
        pl.BlockSpec((None, SEQ, HEAD_DIM), lambda b, h: (b, 0, COL_K // HEAD_DIM + h)),
        pl.BlockSpec((None, SEQ, HEAD_DIM), lambda b, h: (b, 0, COL_V // HEAD_DIM + h)),
    ]
    args = [proj3, proj3, proj3]
    aliases = {}
    if kv_stacks is not None:
        in_specs += [pl.BlockSpec(memory_space=pl.ANY)] * 2
        args += list(kv_stacks)
        aliases = {3: 1, 4: 2}
    out_c, k_stack, v_stack = pl.pallas_call(
        _moba_prompt_kernel,
        out_shape=(jax.ShapeDtypeStruct((BATCH, SEQ, D_MODEL), F32), kv, kv),
        grid=(BATCH, N_KV_HEADS),
        in_specs=in_specs,
        out_specs=(pl.BlockSpec((None, SEQ, gw), lambda b, h: (b, 0, h)), kv_spec, kv_spec),
        input_output_aliases=aliases,
        compiler_params=_params("parallel", "arbitrary"),
    )(*args)
    return out_c, (k_stack, v_stack)


def _moba_decode_kernel(pt_ref, q_ref, kn_ref, vn_ref, *refs, n_pages):
    del pt_ref
    assert SUBLANES == 2 * N_KV_HEADS
    k_pages = refs[:n_pages]
    v_pages = refs[n_pages:2 * n_pages]
    o_ref = refs[2 * n_pages]
    per = MOBA_BLOCK // PAGE_SIZE
    nb = n_pages // per
    nt = (((1,), (1,)), ((), ()))
    q = q_ref[...]
    qs = (q * ATTN_SCALE).astype(BF16)
    head_kv = lax.broadcasted_iota(jnp.int32, (N_HEADS, PAGE_ROWS), 0) >> KV_GROUP_SHIFT
    row_kv = lax.broadcasted_iota(jnp.int32, (N_HEADS, PAGE_ROWS), 1) & (N_KV_HEADS - 1)
    own_kv = row_kv == head_kv
    scores, ksums = [], []
    for pg in range(n_pages):
        kp = k_pages[pg][...]
        scores.append(lax.dot_general(qs, kp.astype(BF16), nt, preferred_element_type=F32))
        ksums.append(jnp.sum(kp.reshape(PAGE_ROWS // SUBLANES, SUBLANES, HEAD_DIM), axis=0))
    kmeans = []
    for j in range(nb):
        ks = ksums[per * j]
        for r in range(1, per):
            ks = ks + ksums[per * j + r]
        ks = ks + pltpu.roll(ks, N_KV_HEADS, 0)
        kmeans.append(ks * (1.0 / MOBA_BLOCK))
    kmean = jnp.concatenate(kmeans, axis=0).astype(BF16)
    gate = lax.dot_general(q.astype(BF16), kmean, nt, preferred_element_type=F32)
    n_cand = nb * SUBLANES
    cand = lax.broadcasted_iota(jnp.int32, (N_HEADS, n_cand), 1) & (SUBLANES - 1)
    cand_head_kv = lax.broadcasted_iota(jnp.int32, (N_HEADS, n_cand), 0) >> KV_GROUP_SHIFT
    sel = _topk_select(gate, cand == cand_head_kv, n_cand)
    masked = []
    for j in range(nb):
        on = jnp.max(sel[:, j * SUBLANES:(j + 1) * SUBLANES], axis=-1, keepdims=True) > 0.0
        for r in range(per):
            masked.append(jnp.where(own_kv, jnp.where(on, scores[per * j + r], NEG), NEG))
    kn = kn_ref[...]
    vn = vn_ref[...]
    kn_h = jnp.concatenate([kn[h // KV_GROUP:h // KV_GROUP + 1] for h in range(N_HEADS)], axis=0)
    vn_h = jnp.concatenate([vn[h // KV_GROUP:h // KV_GROUP + 1] for h in range(N_HEADS)], axis=0)
    s_own = jnp.sum(q * ATTN_SCALE * kn_h, axis=-1, keepdims=True)
    m = jnp.maximum(jnp.max(functools.reduce(jnp.maximum, masked), axis=-1, keepdims=True), s_own)
    p_own = jnp.exp(s_own - m)
    acc = p_own * vn_h
    p_sum = jnp.zeros((N_HEADS, PAGE_ROWS), F32)
    for pg in range(n_pages):
        p = jnp.exp(masked[pg] - m)
        p_sum = p_sum + p
        acc = acc + jnp.dot(p.astype(BF16), v_pages[pg][...].astype(BF16), preferred_element_type=F32)
    l = jnp.sum(p_sum, axis=-1, keepdims=True) + p_own
    o_ref[...] = acc / l


def _moba_decode(proj, cache_k, cache_v, page_table, layer):
    m = proj.shape[0]
    n_pages = page_table.shape[1]
    q = proj[:, COL_Q:COL_Q + D_MODEL].reshape(m, N_HEADS, HEAD_DIM)
    kn = proj[:, COL_K:COL_K + KV_DIM].reshape(m, N_KV_HEADS, HEAD_DIM)
    vn = proj[:, COL_V:COL_V + KV_DIM].reshape(m, N_KV_HEADS, HEAD_DIM)

    def page_spec(pg):
        return pl.BlockSpec((None, None, PAGE_ROWS, HEAD_DIM), lambda b, pt: (layer, pt[b, pg], 0, 0))

    grid_spec = pltpu.PrefetchScalarGridSpec(
        num_scalar_prefetch=1,
        grid=(m,),
        in_specs=[
            pl.BlockSpec((None, N_HEADS, HEAD_DIM), lambda b, pt: (b, 0, 0)),
            pl.BlockSpec((None, N_KV_HEADS, HEAD_DIM), lambda b, pt: (b, 0, 0)),
            pl.BlockSpec((None, N_KV_HEADS, HEAD_DIM), lambda b, pt: (b, 0, 0)),
        ] + [page_spec(pg) for pg in range(n_pages)] + [page_spec(pg) for pg in range(n_pages)],
        out_specs=pl.BlockSpec((None, N_HEADS, HEAD_DIM), lambda b, pt: (b, 0, 0)),
    )
    out = pl.pallas_call(
        functools.partial(_moba_decode_kernel, n_pages=n_pages),
        out_shape=jax.ShapeDtypeStruct((m, N_HEADS, HEAD_DIM), F32),
        grid_spec=grid_spec,
        compiler_params=_params("parallel"),
    )(page_table, q, kn, vn, *([cache_k] * n_pages), *([cache_v] * n_pages))
    return out.reshape(m, D_MODEL)


def _merge_tail(x, gate, cb, zc, ga, gb, gc, ob, oc, wc_ref, wo_ref):
    out_a = jnp.dot((cb.astype(F32) * zc).astype(BF16), wc_ref[...], preferred_element_type=F32)
    merged = (jax.nn.sigmoid(ga.astype(F32)) * out_a + jax.nn.sigmoid(gb.astype(F32)) * ob
              + jax.nn.sigmoid(gc.astype(F32)) * oc)
    return x + gate * jnp.dot(merged.astype(BF16), wo_ref[...], preferred_element_type=F32)


def _mix_prompt_kernel(x_ref, gate_ref, cb_ref, cc_ref, cx_ref, ga_ref, gb_ref, gc_ref, ob_ref, oc_ref,
                       cw_ref, wc_ref, wo_ref, xo_ref, last_ref, z_ref, *, tm, tiles_per_seq):
    tail = SUBLANES

    @pl.when(pl.program_id(0) % tiles_per_seq == 0)
    def _():
        z_ref[0:tail, :] = jnp.zeros((tail, D_MODEL), F32)

    z = cc_ref[...].astype(F32) * cx_ref[...].astype(F32)
    z_ref[tail:tail + tm, :] = z
    w = cw_ref[...]
    zc = (z_ref[tail - 2:tail - 2 + tm, :] * w[0:1] + z_ref[tail - 1:tail - 1 + tm, :] * w[1:2] + z * w[2:3])
    z_last = z[tm - tail:, :]
    z_ref[0:tail, :] = z_last
    last_ref[...] = z_last
    xo_ref[...] = _merge_tail(x_ref[...], gate_ref[...], cb_ref[...], zc, ga_ref[...], gb_ref[...], gc_ref[...],
                              ob_ref[...], oc_ref[...], wc_ref, wo_ref)


def _mix_decode_kernel(x_ref, gate_ref, cb_ref, cc_ref, cx_ref, ga_ref, gb_ref, gc_ref, ob_ref, oc_ref,
                       cw_ref, wc_ref, wo_ref, p0_ref, p1_ref, xo_ref, last_ref):
    z = cc_ref[...].astype(F32) * cx_ref[...].astype(F32)
    w = cw_ref[...]
    p1 = p1_ref[...]
    zc = p0_ref[...] * w[0:1] + p1 * w[1:2] + z * w[2:3]
    last_ref[:, 0:D_MODEL] = p1
    last_ref[:, D_MODEL:2 * D_MODEL] = z
    xo_ref[...] = _merge_tail(x_ref[...], gate_ref[...], cb_ref[...], zc, ga_ref[...], gb_ref[...], gc_ref[...],
                              ob_ref[...], oc_ref[...], wc_ref, wo_ref)


def _mix(x, mods, layer, proj, out_b, out_c, conv_w, w_conv_out, w_o, decode, conv_state=None):
    m = x.shape[0]
    tm = m if decode else 256
    gate_arr, gate_spec = _mod_spec(*mods, layer, 2, tm, SEQ, decode)

    def col(c):
        return pl.BlockSpec((tm, D_MODEL), lambda i: (i, c))

    row = pl.BlockSpec((tm, D_MODEL), lambda i: (i, 0))
    wspec = pl.BlockSpec((None, D_MODEL, D_MODEL), lambda i: (layer, 0, 0))
    in_specs = [row, gate_spec, col(COL_CB), col(COL_CC), col(COL_CX), col(COL_GA), col(COL_GB), col(COL_GC),
                row, row, pl.BlockSpec((None, CONV_WIDTH, D_MODEL), lambda i: (layer, 0, 0)), wspec, wspec]
    args = [x, gate_arr, proj, proj, proj, proj, proj, proj, out_b, out_c, conv_w, w_conv_out, w_o]
    if decode:
        state2 = conv_state.reshape(DEPTH, m, (CONV_WIDTH - 1) * D_MODEL)
        in_specs += [pl.BlockSpec((None, m, D_MODEL), lambda i: (layer, 0, 0)),
                     pl.BlockSpec((None, m, D_MODEL), lambda i: (layer, 0, 1))]
        args += [state2, state2]
        return pl.pallas_call(
            _mix_decode_kernel,
            out_shape=(jax.ShapeDtypeStruct((m, D_MODEL), F32),
                       jax.ShapeDtypeStruct((m, (CONV_WIDTH - 1) * D_MODEL), F32)),
            grid=(1,),
            in_specs=in_specs,
            out_specs=(row, pl.BlockSpec((m, (CONV_WIDTH - 1) * D_MODEL), lambda i: (0, 0))),
            compiler_params=_params("arbitrary"),
        )(*args)
    tiles_per_seq = SEQ // tm
    return pl.pallas_call(
        functools.partial(_mix_prompt_kernel, tm=tm, tiles_per_seq=tiles_per_seq),
        out_shape=(jax.ShapeDtypeStruct((m, D_MODEL), F32),
                   jax.ShapeDtypeStruct((BATCH, SUBLANES, D_MODEL), F32)),
        grid=(m // tm,),
        in_specs=in_specs,
        out_specs=(row, pl.BlockSpec((None, SUBLANES, D_MODEL), lambda i: (i // tiles_per_seq, 0, 0))),
        scratch_shapes=[pltpu.VMEM((SUBLANES + tm, D_MODEL), F32)],
        compiler_params=_params("arbitrary"),
    )(*args)


def _ffn_kernel(x_ref, g_ref, sc_ref, sh_ref, gate_ref, w1_ref, w3_ref, w2_ref, o_ref, h_ref, acc_ref):
    f = pl.program_id(1)

    @pl.when(f == 0)
    def _():
        h_ref[...] = _mod_norm(x_ref[...], g_ref[...], sc_ref[...], sh_ref[...]).astype(BF16)
        acc_ref[...] = jnp.zeros_like(acc_ref)

    h = h_ref[...]
    a = (jax.nn.silu(jnp.dot(h, w1_ref[...], preferred_element_type=F32))
         * jnp.dot(h, w3_ref[...], preferred_element_type=F32))
    acc_ref[...] += jnp.dot(a.astype(BF16), w2_ref[...], preferred_element_type=F32)

    @pl.when(f == pl.num_programs(1) - 1)
    def _():
        o_ref[...] = x_ref[...] + gate_ref[...] * acc_ref[...]


def _ffn(x, mods, layer, g_ffn, w1, w3, w2, decode):
    m = x.shape[0]
    tm = m if decode else 512
    fc = D_FF // 2
    sc_arr, sc_spec = _mod_spec(*mods, layer, 4, tm, SEQ, decode)
    sh_arr, sh_spec = _mod_spec(*mods, layer, 3, tm, SEQ, decode)
    gt_arr, gt_spec = _mod_spec(*mods, layer, 5, tm, SEQ, decode)
    row = pl.BlockSpec((tm, D_MODEL), lambda i, f: (i, 0))
    return pl.pallas_call(
        _ffn_kernel,
        out_shape=jax.ShapeDtypeStruct((m, D_MODEL), F32),
        grid=(m // tm, D_FF // fc),
        in_specs=[row, pl.BlockSpec((None, 1, D_MODEL), lambda i, f: (layer, 0, 0)), sc_spec, sh_spec, gt_spec,
                  pl.BlockSpec((None, D_MODEL, fc), lambda i, f: (layer, 0, f)),
                  pl.BlockSpec((None, D_MODEL, fc), lambda i, f: (layer, 0, f)),
                  pl.BlockSpec((None, fc, D_MODEL), lambda i, f: (layer, f, 0))],
        out_specs=row,
        scratch_shapes=[pltpu.VMEM((tm, D_MODEL), BF16), pltpu.VMEM((tm, D_MODEL), F32)],
        compiler_params=_params("parallel", "arbitrary"),
    )(x, g_ffn, sc_arr, sh_arr, gt_arr, w1, w3, w2)


def _final_norm_kernel(x_ref, g_ref, o_ref):
    x = x_ref[...]
    o_ref[...] = x * lax.rsqrt(jnp.mean(x * x, axis=-1, keepdims=True) + EPS) * g_ref[...]


def _final_norm(x, g):
    m = x.shape[0]
    tm = min(m, 1024)
    row = pl.BlockSpec((tm, D_MODEL), lambda i: (i, 0))
    return pl.pallas_call(
        _final_norm_kernel,
        out_shape=jax.ShapeDtypeStruct((m, D_MODEL), F32),
        grid=(m // tm,),
        in_specs=[row, pl.BlockSpec((1, D_MODEL), lambda i: (0, 0))],
        out_specs=row,
        compiler_params=_params("parallel"),
    )(x, g.reshape(1, D_MODEL))


def _trunk(x, mods, w, decode, state=None):
    ks, vs, convs, res, ims = [], [], [], [], []
    kv_stacks = None
    for l in range(DEPTH):
        mix = _in_proj(x, mods, l, w["g_mix"], w["w_in"], decode, qkv=False)
        qkv = _in_proj(x, mods, l, w["g_mix"], w["w_in"], decode, qkv=True)
        if decode:
            out_b, s_re, s_im = _s5_decode(mix, state["ssm_re"], state["ssm_im"], l, w["s5"])
            out_c = _moba_decode(qkv, state["cache_k"], state["cache_v"], state["page_table"], l)
            ks.append(qkv[:, COL_K:COL_K + KV_DIM])
            vs.append(qkv[:, COL_V:COL_V + KV_DIM])
            x, conv_last = _mix(x, mods, l, mix, out_b, out_c, w["conv_w"], w["w_conv_out"], w["w_o"], True,
                                state["conv"])
            conv_last = conv_last.reshape(DEC_BATCH, CONV_WIDTH - 1, D_MODEL)
        else:
            out_b, s_re, s_im = _s5_prompt(mix.reshape(BATCH, SEQ, D_MIX), l, w["s5"])
            out_c, kv_stacks = _moba_prompt(qkv.reshape(BATCH, SEQ, D_QKV), l, kv_stacks)
            x, conv_last = _mix(x, mods, l, mix, out_b.reshape(BATCH * SEQ, D_MODEL),
                                out_c.reshape(BATCH * SEQ, D_MODEL), w["conv_w"], w["w_conv_out"], w["w_o"], False)
            conv_last = conv_last[:, SUBLANES - (CONV_WIDTH - 1):, :]
        x = _ffn(x, mods, l, w["g_ffn"], w["w_ff1"], w["w_ff3"], w["w_ff2"], decode)
        convs.append(conv_last)
        res.append(s_re)
        ims.append(s_im)
    y = _final_norm(x, w["g_final"])
    k_all, v_all = (jnp.stack(ks), jnp.stack(vs)) if decode else kv_stacks
    return y, k_all, v_all, jnp.stack(convs), jnp.stack(res), jnp.stack(ims)


def kernel(x_prompt, x_sample, cache_k, cache_v, state_conv, state_ssm_re, state_ssm_im, page_table, c_prompt, c_sample, w_ada, b_ada, g_mix, w_in, conv_w, w_conv_out, ssm_lam_re, ssm_lam_im, ssm_log_step, ssm_b_re, ssm_b_im, ssm_c_re, ssm_c_im, ssm_d, w_glu_a, w_glu_b, w_o, g_ffn, w_ff1, w_ff3, w_ff2, g_final):
    n_pool = cache_k.shape[1]
    mod = _ada(jnp.concatenate([c_sample, c_prompt], axis=0), w_ada, b_ada)
    mods = (mod, mod.reshape(DEPTH, DEC_BATCH + BATCH, 6, 1, D_MODEL))

    lam_re, lam_im, bb_re, bb_im = _s5_discretize(
        ssm_lam_re, ssm_lam_im, ssm_log_step, jnp.swapaxes(ssm_b_re, 2, 3), jnp.swapaxes(ssm_b_im, 2, 3))
    s5w = (lam_re.reshape(DEPTH, 1, SSM_LANES), lam_im.reshape(DEPTH, 1, SSM_LANES),
           _block_diag_in(bb_re).astype(BF16), _block_diag_in(bb_im).astype(BF16),
           _block_diag_out(ssm_c_re).astype(BF16), _block_diag_out(ssm_c_im).astype(BF16),
           ssm_d.reshape(DEPTH, 1, D_MODEL), w_glu_a.astype(BF16), w_glu_b.astype(BF16))
    w = dict(g_mix=g_mix.reshape(DEPTH, 1, D_MODEL), w_in=w_in.astype(BF16), s5=s5w, conv_w=conv_w,
             w_conv_out=w_conv_out.astype(BF16), w_o=w_o.astype(BF16), g_ffn=g_ffn.reshape(DEPTH, 1, D_MODEL),
             w_ff1=w_ff1.astype(BF16), w_ff3=w_ff3.astype(BF16), w_ff2=w_ff2.astype(BF16), g_final=g_final)

    yp, kp, vp, cp, rp, ip = _trunk(x_prompt.reshape(BATCH * SEQ, D_MODEL), mods, w, False)
    state = dict(ssm_re=state_ssm_re.reshape(DEPTH, DEC_BATCH, SSM_LANES),
                 ssm_im=state_ssm_im.reshape(DEPTH, DEC_BATCH, SSM_LANES),
                 cache_k=cache_k.reshape(DEPTH, n_pool, PAGE_ROWS, HEAD_DIM),
                 cache_v=cache_v.reshape(DEPTH, n_pool, PAGE_ROWS, HEAD_DIM),
                 page_table=page_table, conv=state_conv)
    ys, ksm, vsm, cs, rs, is_ = _trunk(x_sample.reshape(DEC_BATCH, D_MODEL), mods, w, True, state)

    kv_p = (DEPTH, BATCH, SEQ, N_KV_HEADS, HEAD_DIM)
    kv_s = (DEPTH, DEC_BATCH, 1, N_KV_HEADS, HEAD_DIM)
    st_p = (DEPTH, BATCH, SSM_GROUPS, SSM_STATE)
    st_s = (DEPTH, DEC_BATCH, SSM_GROUPS, SSM_STATE)
    return (yp.reshape(BATCH, SEQ, D_MODEL), ys.reshape(DEC_BATCH, 1, D_MODEL),
            kp.reshape(kv_p), vp.reshape(kv_p), cp, rp.reshape(st_p), ip.reshape(st_p),
            ksm.reshape(kv_s), vsm.reshape(kv_s), cs, rs.reshape(st_s), is_.reshape(st_s))
```

```python
import functools

import jax
import jax.numpy as jnp
from jax import lax
from jax.experimental import pallas as pl
from jax.experimental.pallas import tpu as pltpu

F32 = jnp.float32
BF16 = jnp.bfloat16

D_MODEL = 1024
DEPTH = 4
SEQ = 2048
BATCH = 8
DEC_BATCH = 128
PAGE_SIZE = 128
EPS = 1e-6
CONV_WIDTH = 3
SSM_GROUP = 16
SSM_GROUPS = D_MODEL // SSM_GROUP
SSM_STATE = 64
SSM_LANES = SSM_GROUPS * SSM_STATE
HEAD_DIM = 128
N_HEADS = D_MODEL // HEAD_DIM
N_KV_HEADS = N_HEADS // 2
KV_GROUP = N_HEADS // N_KV_HEADS
KV_DIM = N_KV_HEADS * HEAD_DIM
KV_GROUP_SHIFT = KV_GROUP.bit_length() - 1
assert KV_GROUP == 1 << KV_GROUP_SHIFT and N_KV_HEADS & (N_KV_HEADS - 1) == 0
PAGE_ROWS = PAGE_SIZE * N_KV_HEADS
MOBA_BLOCK = 256
MOBA_TOPK = 3
ATTN_SCALE = HEAD_DIM ** -0.5
NEG = -1e30
D_FF = 2816
D_IN = 9216
COL_CB, COL_CC, COL_CX, COL_U, COL_GA, COL_GB, COL_GC = 0, 1, 2, 3, 4, 5, 6
D_MIX = 7 * D_MODEL
COL_Q = 0
COL_K = D_MODEL
COL_V = COL_K + KV_DIM
D_QKV = D_MODEL + 2 * KV_DIM
W_QKV = 4 * D_MODEL

LANES = 128
SUBLANES = 8
MXU_DIM = 256
VMEM_LIMIT = 56 * 1024 * 1024

S5_SLABS = D_MODEL // MXU_DIM
S5_SLAB_LANES = SSM_LANES // S5_SLABS


def _params(*sem):
    return pltpu.CompilerParams(dimension_semantics=sem, vmem_limit_bytes=VMEM_LIMIT)


def _ada_kernel(c_ref, w_ref, b_ref, o_ref):
    c = jax.nn.silu(c_ref[...]).astype(BF16)
    o_ref[...] = jnp.dot(c, w_ref[...].astype(BF16), preferred_element_type=F32) + b_ref[...]


def _ada(c_all, w_ada, b_ada):
    n_rows = c_all.shape[0]
    tn = 1536
    return pl.pallas_call(
        _ada_kernel,
        out_shape=jax.ShapeDtypeStruct((DEPTH, n_rows, 6 * D_MODEL), F32),
        grid=(DEPTH, 6 * D_MODEL // tn),
        in_specs=[
            pl.BlockSpec((n_rows, D_MODEL), lambda l, j: (0, 0)),
            pl.BlockSpec((None, D_MODEL, tn), lambda l, j: (l, 0, j)),
            pl.BlockSpec((None, 1, tn), lambda l, j: (l, 0, j)),
        ],
        out_specs=pl.BlockSpec((None, n_rows, tn), lambda l, j: (l, 0, j)),
        compiler_params=_params("parallel", "parallel"),
    )(c_all, w_ada, b_ada.reshape(DEPTH, 1, 6 * D_MODEL))


def _s5_disc_kernel(lre_ref, lim_ref, ls_ref, bre_ref, bim_ref, lbre_ref, lbim_ref, bbre_ref, bbim_ref):
    lre = lre_ref[...]
    lim = lim_ref[...]
    step = jnp.exp(ls_ref[...])
    mag = jnp.exp(lre * step)
    ang = lim * step
    lbre = mag * jnp.cos(ang)
    lbim = mag * jnp.sin(ang)
    lbre_ref[...] = lbre
    lbim_ref[...] = lbim
    nre = lbre - 1.0
    den = lre * lre + lim * lim
    cre = (nre * lre + lbim * lim) / den
    cim = (lbim * lre - nre * lim) / den
    bre = bre_ref[...]
    bim = bim_ref[...]
    bbre_ref[...] = cre * bre - cim * bim
    bbim_ref[...] = cre * bim + cim * bre


def _s5_discretize(lam_re, lam_im, log_step, b_re_t, b_im_t):
    g, p, i = SSM_GROUPS, SSM_STATE, SSM_GROUP
    gp = pl.BlockSpec((None, g, 1, p), lambda l: (l, 0, 0, 0))
    gip = pl.BlockSpec((None, g, i, p), lambda l: (l, 0, 0, 0))
    return pl.pallas_call(
        _s5_disc_kernel,
        out_shape=(jax.ShapeDtypeStruct((DEPTH, g, 1, p), F32), jax.ShapeDtypeStruct((DEPTH, g, 1, p), F32),
                   jax.ShapeDtypeStruct((DEPTH, g, i, p), F32), jax.ShapeDtypeStruct((DEPTH, g, i, p), F32)),
        grid=(DEPTH,),
        in_specs=[gp, gp, pl.BlockSpec((None, g, 1, 1), lambda l: (l, 0, 0, 0)), gip, gip],
        out_specs=(gp, gp, gip, gip),
        compiler_params=_params("parallel"),
    )(lam_re.reshape(DEPTH, g, 1, p), lam_im.reshape(DEPTH, g, 1, p), log_step.reshape(DEPTH, g, 1, 1),
      b_re_t, b_im_t)


def _block_diag_in(b_t):
    return _tile_diag(b_t.reshape(DEPTH, S5_SLABS, MXU_DIM, SSM_STATE), SSM_GROUP, SSM_STATE)


def _block_diag_out(c):
    ct = jnp.swapaxes(c, 2, 3).reshape(DEPTH, S5_SLABS, S5_SLAB_LANES, SSM_GROUP)
    return _tile_diag(ct, SSM_STATE, SSM_GROUP)


def _tile_diag(a, rows_per_group, cols_per_group):
    gs = SSM_GROUPS // S5_SLABS
    tiled = jnp.tile(a, (1, 1, 1, gs))
    row_g = jnp.arange(gs * rows_per_group)[:, None] // rows_per_group
    col_g = jnp.arange(gs * cols_per_group)[None, :] // cols_per_group
    return jnp.where(row_g == col_g, tiled, 0.0)


def _mod_norm(x, g, scale, shift):
    y = x * lax.rsqrt(jnp.mean(x * x, axis=-1, keepdims=True) + EPS)
    return (y * g) * (1.0 + scale) + shift


IN_PROJ_CHUNK = 1024


def _in_proj_kernel(x_ref, g_ref, sc_ref, sh_ref, w_ref, mix_ref, qkv_ref):
    h = _mod_norm(x_ref[...], g_ref[...], sc_ref[...], sh_ref[...]).astype(BF16)
    nc = IN_PROJ_CHUNK
    for n in range(D_MIX // nc):
        src = n * nc if n * nc < W_QKV else n * nc + D_QKV
        mix_ref[:, n * nc:(n + 1) * nc] = jnp.dot(
            h, w_ref[:, src:src + nc], preferred_element_type=F32).astype(BF16)
    for n in range(D_QKV // nc):
        src = W_QKV + n * nc
        qkv_ref[:, n * nc:(n + 1) * nc] = jnp.dot(h, w_ref[:, src:src + nc], preferred_element_type=F32)


def _mod_spec(mod, mod5, layer, chunk, tm, rows_per_seq, decode):
    if decode:
        return mod, pl.BlockSpec((None, tm, D_MODEL), lambda i, *_: (layer, 0, chunk))
    return mod5, pl.BlockSpec((None, None, None, 1, D_MODEL),
                              lambda i, *_: (layer, DEC_BATCH + (i * tm) // rows_per_seq, chunk, 0, 0))


def _in_proj(x, mods, layer, g_mix, w_in, decode):
    m = x.shape[0]
    tm = m if decode else 512
    sc_arr, sc_spec = _mod_spec(*mods, layer, 1, tm, SEQ, decode)
    sh_arr, sh_spec = _mod_spec(*mods, layer, 0, tm, SEQ, decode)
    return pl.pallas_call(
        _in_proj_kernel,
        out_shape=(jax.ShapeDtypeStruct((m, D_MIX), BF16), jax.ShapeDtypeStruct((m, D_QKV), F32)),
        grid=(m // tm,),
        in_specs=[
            pl.BlockSpec((tm, D_MODEL), lambda i: (i, 0)),
            pl.BlockSpec((None, 1, D_MODEL), lambda i: (layer, 0, 0)),
            sc_spec, sh_spec,
            pl.BlockSpec((None, D_MODEL, D_IN), lambda i: (layer, 0, 0), pipeline_mode=pl.Buffered(1)),
        ],
        out_specs=(pl.BlockSpec((tm, D_MIX), lambda i: (i, 0)), pl.BlockSpec((tm, D_QKV), lambda i: (i, 0))),
        compiler_params=_params("parallel"),
    )(x, g_mix, sc_arr, sh_arr, w_in)


def _s5_readout(u_slab, sre_slab, sim_slab, cre_ref, cim_ref, d_ref, s):
    y = (jnp.dot(sre_slab.astype(BF16), cre_ref[s], preferred_element_type=F32)
         - jnp.dot(sim_slab.astype(BF16), cim_ref[s], preferred_element_type=F32))
    y = y + u_slab * d_ref[:, s * MXU_DIM:(s + 1) * MXU_DIM]
    return jax.nn.gelu(y, approximate=True)


def _glu(g, wa_ref, wb_ref):
    gb = g.astype(BF16)
    return (jnp.dot(gb, wa_ref[...], preferred_element_type=F32)
            * jax.nn.sigmoid(jnp.dot(gb, wb_ref[...], preferred_element_type=F32)))


SCAN_LANES = 512


def _s5_prompt_kernel(u_ref, lre_ref, lim_ref, bre_ref, bim_ref, cre_ref, cim_ref, d_ref, wa_ref, wb_ref,
                      ob_ref, fre_ref, fim_ref,
                      us_ref, sre_ref, sim_ref, os_ref, st_ref, *, tt):
    n_lane_slabs = D_MODEL // LANES

    @pl.when(pl.program_id(0) == 0)
    def _():
        st_ref[...] = jnp.zeros_like(st_ref)

    for b in range(BATCH):
        for sl in range(n_lane_slabs):
            us_ref[sl, pl.ds(b, tt, stride=BATCH), :] = u_ref[b, :, sl * LANES:(sl + 1) * LANES].astype(F32)

    def u_slab(s):
        per = MXU_DIM // LANES
        return jnp.concatenate([us_ref[per * s + k] for k in range(per)], axis=1)

    gs = []
    for s in range(S5_SLABS):
        ub = u_slab(s).astype(BF16)
        slab = slice(s * S5_SLAB_LANES, (s + 1) * S5_SLAB_LANES)
        sre_ref[:, slab] = jnp.dot(ub, bre_ref[s], preferred_element_type=F32)
        sim_ref[:, slab] = jnp.dot(ub, bim_ref[s], preferred_element_type=F32)
        for c in range(S5_SLAB_LANES // SCAN_LANES):
            cols = slice(s * S5_SLAB_LANES + c * SCAN_LANES, s * S5_SLAB_LANES + (c + 1) * SCAN_LANES)
            lre = jnp.broadcast_to(lre_ref[:, cols], (BATCH, SCAN_LANES))
            lim = jnp.broadcast_to(lim_ref[:, cols], (BATCH, SCAN_LANES))
            pre, pim = st_ref[0, :, cols], st_ref[1, :, cols]
            for t in range(tt):
                rows = slice(t * BATCH, (t + 1) * BATCH)
                pre, pim = (lre * pre - lim * pim + sre_ref[rows, cols],
                            lre * pim + lim * pre + sim_ref[rows, cols])
                sre_ref[rows, cols] = pre
                sim_ref[rows, cols] = pim
            st_ref[0, :, cols] = pre
            st_ref[1, :, cols] = pim
        gs.append(_s5_readout(u_slab(s), sre_ref[:, slab], sim_ref[:, slab], cre_ref, cim_ref, d_ref, s))

    fre_ref[...] = st_ref[0]
    fim_ref[...] = st_ref[1]

    ob = _glu(jnp.concatenate(gs, axis=1), wa_ref, wb_ref)

    for sl in range(n_lane_slabs):
        os_ref[sl] = ob[:, sl * LANES:(sl + 1) * LANES]
    for b in range(BATCH):
        for sl in range(n_lane_slabs):
            ob_ref[b, :, sl * LANES:(sl + 1) * LANES] = os_ref[sl, pl.ds(b, tt, stride=BATCH), :].astype(BF16)


def _s5_decode_kernel(u_ref, s0re_ref, s0im_ref, lre_ref, lim_ref, bre_ref, bim_ref, cre_ref, cim_ref, d_ref,
                      wa_ref, wb_ref, ob_ref, fre_ref, fim_ref):
    gs = []
    for s in range(S5_SLABS):
        cols = slice(s * S5_SLAB_LANES, (s + 1) * S5_SLAB_LANES)
        ub = u_ref[:, s * MXU_DIM:(s + 1) * MXU_DIM]
        u = ub.astype(F32)
        lre, lim = lre_ref[:, cols], lim_ref[:, cols]
        pre, pim = s0re_ref[:, cols], s0im_ref[:, cols]
        nre = lre * pre - lim * pim + jnp.dot(ub, bre_ref[s], preferred_element_type=F32)
        nim = lre * pim + lim * pre + jnp.dot(ub, bim_ref[s], preferred_element_type=F32)
        fre_ref[:, cols] = nre
        fim_ref[:, cols] = nim
        gs.append(_s5_readout(u, nre, nim, cre_ref, cim_ref, d_ref, s))
    ob_ref[...] = _glu(jnp.concatenate(gs, axis=1), wa_ref, wb_ref)


def _s5_weight_specs(layer, n_grid):
    def const(shape, *idx):
        return pl.BlockSpec(shape, lambda *_: idx, pipeline_mode=pl.Buffered(1))

    return [
        const((None, 1, SSM_LANES), layer, 0, 0),
        const((None, 1, SSM_LANES), layer, 0, 0),
        const((None, S5_SLABS, MXU_DIM, S5_SLAB_LANES), layer, 0, 0, 0),
        const((None, S5_SLABS, MXU_DIM, S5_SLAB_LANES), layer, 0, 0, 0),
        const((None, S5_SLABS, S5_SLAB_LANES, MXU_DIM), layer, 0, 0, 0),
        const((None, S5_SLABS, S5_SLAB_LANES, MXU_DIM), layer, 0, 0, 0),
        const((None, 1, D_MODEL), layer, 0, 0),
        const((None, D_MODEL, D_MODEL), layer, 0, 0),
        const((None, D_MODEL, D_MODEL), layer, 0, 0),
    ]


def _s5_prompt(mix3, layer, s5w):
    tt = 64
    rows = tt * BATCH
    kern = functools.partial(_s5_prompt_kernel, tt=tt)
    state = jax.ShapeDtypeStruct((BATCH, SSM_LANES), F32)
    state_spec = pl.BlockSpec((BATCH, SSM_LANES), lambda i: (0, 0))
    return pl.pallas_call(
        kern,
        out_shape=(jax.ShapeDtypeStruct((BATCH, SEQ, D_MODEL), BF16), state, state),
        grid=(SEQ // tt,),
        in_specs=[pl.BlockSpec((BATCH, tt, D_MODEL), lambda i: (0, i, COL_U))] + _s5_weight_specs(layer, 1),
        out_specs=(pl.BlockSpec((BATCH, tt, D_MODEL), lambda i: (0, i, 0)), state_spec, state_spec),
        scratch_shapes=[
            pltpu.VMEM((D_MODEL // LANES, rows, LANES), F32),
            pltpu.VMEM((rows, SSM_LANES), F32),
            pltpu.VMEM((rows, SSM_LANES), F32),
            pltpu.VMEM((D_MODEL // LANES, rows, LANES), F32),
            pltpu.VMEM((2, BATCH, SSM_LANES), F32),
        ],
        compiler_params=_params("arbitrary"),
    )(mix3, *s5w)


def _s5_decode(mix, s0re, s0im, layer, s5w):
    m = mix.shape[0]
    state = jax.ShapeDtypeStruct((m, SSM_LANES), F32)
    full = pl.BlockSpec((m, SSM_LANES), lambda i: (0, 0))
    return pl.pallas_call(
        _s5_decode_kernel,
        out_shape=(jax.ShapeDtypeStruct((m, D_MODEL), F32), state, state),
        grid=(1,),
        in_specs=[pl.BlockSpec((m, D_MODEL), lambda i: (0, COL_U)),
                  pl.BlockSpec((None, m, SSM_LANES), lambda i: (layer, 0, 0)),
                  pl.BlockSpec((None, m, SSM_LANES), lambda i: (layer, 0, 0))] + _s5_weight_specs(layer, 1),
        out_specs=(pl.BlockSpec((m, D_MODEL), lambda i: (0, 0)), full, full),
        compiler_params=_params("arbitrary"),
    )(mix, s0re, s0im, *s5w)


def _topk_select(gate, valid, n_candidates):
    lane = lax.broadcasted_iota(jnp.int32, gate.shape, 1)
    g = jnp.where(valid, gate, NEG)
    rank = jnp.zeros(gate.shape, jnp.int32)
    for jp in range(n_candidates):
        col = g[:, jp:jp + 1]
        rank = rank + jnp.where(col > g, 1, jnp.where(col == g, jnp.where(lane > jp, 1, 0), 0))
    return jnp.where(valid, jnp.where(rank < MOBA_TOPK, 1.0, 0.0), 0.0)


def _moba_prompt_kernel(q_ref, k_ref, v_ref, *refs):
    o_ref, ko_ref, vo_ref = refs[-3:]
    nb = SEQ // MOBA_BLOCK
    k = k_ref[...]
    v = v_ref[...]
    head_rows = pl.ds(pl.program_id(1), SEQ, stride=N_KV_HEADS)
    ko_ref[head_rows, :] = k
    vo_ref[head_rows, :] = v
    kb = k.astype(BF16)
    vb = v.astype(BF16)
    kmean = jnp.mean(k.reshape(nb, MOBA_BLOCK, HEAD_DIM), axis=1).astype(BF16)
    rows = KV_GROUP * MOBA_BLOCK
    lane = lax.broadcasted_iota(jnp.int32, (rows, nb), 1)
    q_in_blk = lax.broadcasted_iota(jnp.int32, (rows, MOBA_BLOCK), 0) & (MOBA_BLOCK - 1)
    k_in_blk = lax.broadcasted_iota(jnp.int32, (rows, MOBA_BLOCK), 1)
    causal = k_in_blk <= q_in_blk
    nt = (((1,), (1,)), ((), ()))
    for i in range(nb):
        q2 = q_ref[i * MOBA_BLOCK:(i + 1) * MOBA_BLOCK, :]
        q = jnp.concatenate([q2[:, h * HEAD_DIM:(h + 1) * HEAD_DIM] for h in range(KV_GROUP)], axis=0)
        nk = (i + 1) * MOBA_BLOCK
        s = lax.dot_general((q * ATTN_SCALE).astype(BF16), kb[:nk], nt, preferred_element_type=F32)
        parts = []
        if i > 0:
            gate = lax.dot_general(q.astype(BF16), kmean, nt, preferred_element_type=F32)
            sel = _topk_select(gate, lane < i, i)
            for j in range(i):
                parts.append(jnp.where(sel[:, j:j + 1] > 0.0, s[:, j * MOBA_BLOCK:(j + 1) * MOBA_BLOCK], NEG))
        parts.append(jnp.where(causal, s[:, i * MOBA_BLOCK:], NEG))
        s = jnp.concatenate(parts, axis=1) if len(parts) > 1 else parts[0]
        m = jnp.max(s, axis=-1, keepdims=True)
        p = jnp.exp(s - m)
        l = jnp.sum(p, axis=-1, keepdims=True)
        o = jnp.dot(p.astype(BF16), vb[:nk], preferred_element_type=F32) / l
        for h in range(KV_GROUP):
            o_ref[i * MOBA_BLOCK:(i + 1) * MOBA_BLOCK, h * HEAD_DIM:(h + 1) * HEAD_DIM] = (
                o[h * MOBA_BLOCK:(h + 1) * MOBA_BLOCK].astype(BF16))


def _moba_prompt(proj3, layer, kv_stacks):
    gw = KV_GROUP * HEAD_DIM
    kv = jax.ShapeDtypeStruct((DEPTH, BATCH, SEQ * N_KV_HEADS, HEAD_DIM), F32)
    kv_spec = pl.BlockSpec((None, None, SEQ * N_KV_HEADS, HEAD_DIM), lambda b, h: (layer, b, 0, 0))
    in_specs = [
        pl.BlockSpec((None, SEQ, gw), lambda b, h: (b, 0, COL_Q // gw + h)),
        pl.BlockSpec((None, SEQ, HEAD_DIM), lambda b, h: (b, 0, COL_K // HEAD_DIM + h)),
        pl.BlockSpec((None, SEQ, HEAD_DIM), lambda b, h: (b, 0, COL_V // HEAD_DIM + h)),
    ]
    args = [proj3, proj3, proj3]
    aliases = {}
    if kv_stacks is not None:
        in_specs += [pl.BlockSpec(memory_space=pl.ANY)] * 2
        args += list(kv_stacks)
        aliases = {3: 1, 4: 2}
    out_c, k_stack, v_stack = pl.pallas_call(
        _moba_prompt_kernel,
        out_shape=(jax.ShapeDtypeStruct((BATCH, SEQ, D_MODEL), BF16), kv, kv),
        grid=(BATCH, N_KV_HEADS),
        in_specs=in_specs,
        out_specs=(pl.BlockSpec((None, SEQ, gw), lambda b, h: (b, 0, h)), kv_spec, kv_spec),
        input_output_aliases=aliases,
        compiler_params=_params("parallel", "arbitrary"),
    )(*args)
    return out_c, (k_stack, v_stack)


def _moba_decode_kernel(pt_ref, q_ref, kn_ref, vn_ref, *refs, n_pages):
    del pt_ref
    assert SUBLANES == 2 * N_KV_HEADS
    k_pages = refs[:n_pages]
    v_pages = refs[n_pages:2 * n_pages]
    o_ref = refs[2 * n_pages]
    per = MOBA_BLOCK // PAGE_SIZE
    nb = n_pages // per
    nt = (((1,), (1,)), ((), ()))
    q = q_ref[...]
    qs = (q * ATTN_SCALE).astype(BF16)
    head_kv = lax.broadcasted_iota(jnp.int32, (N_HEADS, PAGE_ROWS), 0) >> KV_GROUP_SHIFT
    row_kv = lax.broadcasted_iota(jnp.int32, (N_HEADS, PAGE_ROWS), 1) & (N_KV_HEADS - 1)
    own_kv = row_kv == head_kv
    scores, ksums = [], []
    for pg in range(n_pages):
        kp = k_pages[pg][...]
        scores.append(lax.dot_general(qs, kp.astype(BF16), nt, preferred_element_type=F32))
        ksums.append(jnp.sum(kp.reshape(PAGE_ROWS // SUBLANES, SUBLANES, HEAD_DIM), axis=0))
    kmeans = []
    for j in range(nb):
        ks = ksums[per * j]
        for r in range(1, per):
            ks = ks + ksums[per * j + r]
        ks = ks + pltpu.roll(ks, N_KV_HEADS, 0)
        kmeans.append(ks * (1.0 / MOBA_BLOCK))
    kmean = jnp.concatenate(kmeans, axis=0).astype(BF16)
    gate = lax.dot_general(q.astype(BF16), kmean, nt, preferred_element_type=F32)
    n_cand = nb * SUBLANES
    cand = lax.broadcasted_iota(jnp.int32, (N_HEADS, n_cand), 1) & (SUBLANES - 1)
    cand_head_kv = lax.broadcasted_iota(jnp.int32, (N_HEADS, n_cand), 0) >> KV_GROUP_SHIFT
    sel = _topk_select(gate, cand == cand_head_kv, n_cand)
    masked = []
    for j in range(nb):
        on = jnp.max(sel[:, j * SUBLANES:(j + 1) * SUBLANES], axis=-1, keepdims=True) > 0.0
        for r in range(per):
            masked.append(jnp.where(own_kv, jnp.where(on, scores[per * j + r], NEG), NEG))
    kn = kn_ref[...]
    vn = vn_ref[...]
    kn_h = jnp.concatenate([kn[h // KV_GROUP:h // KV_GROUP + 1] for h in range(N_HEADS)], axis=0)
    vn_h = jnp.concatenate([vn[h // KV_GROUP:h // KV_GROUP + 1] for h in range(N_HEADS)], axis=0)
    s_own = jnp.sum(q * ATTN_SCALE * kn_h, axis=-1, keepdims=True)
    m = jnp.maximum(jnp.max(functools.reduce(jnp.maximum, masked), axis=-1, keepdims=True), s_own)
    p_own = jnp.exp(s_own - m)
    acc = p_own * vn_h
    p_sum = jnp.zeros((N_HEADS, PAGE_ROWS), F32)
    for pg in range(n_pages):
        p = jnp.exp(masked[pg] - m)
        p_sum = p_sum + p
        acc = acc + jnp.dot(p.astype(BF16), v_pages[pg][...].astype(BF16), preferred_element_type=F32)
    l = jnp.sum(p_sum, axis=-1, keepdims=True) + p_own
    o_ref[...] = acc / l


def _moba_decode(proj, cache_k, cache_v, page_table, layer):
    m = proj.shape[0]
    n_pages = page_table.shape[1]
    q = proj[:, COL_Q:COL_Q + D_MODEL].reshape(m, N_HEADS, HEAD_DIM)
    kn = proj[:, COL_K:COL_K + KV_DIM].reshape(m, N_KV_HEADS, HEAD_DIM)
    vn = proj[:, COL_V:COL_V + KV_DIM].reshape(m, N_KV_HEADS, HEAD_DIM)

    def page_spec(pg):
        return pl.BlockSpec((None, None, PAGE_ROWS, HEAD_DIM), lambda b, pt: (layer, pt[b, pg], 0, 0))

    grid_spec = pltpu.PrefetchScalarGridSpec(
        num_scalar_prefetch=1,
        grid=(m,),
        in_specs=[
            pl.BlockSpec((None, N_HEADS, HEAD_DIM), lambda b, pt: (b, 0, 0)),
            pl.BlockSpec((None, N_KV_HEADS, HEAD_DIM), lambda b, pt: (b, 0, 0)),
            pl.BlockSpec((None, N_KV_HEADS, HEAD_DIM), lambda b, pt: (b, 0, 0)),
        ] + [page_spec(pg) for pg in range(n_pages)] + [page_spec(pg) for pg in range(n_pages)],
        out_specs=pl.BlockSpec((None, N_HEADS, HEAD_DIM), lambda b, pt: (b, 0, 0)),
    )
    out = pl.pallas_call(
        functools.partial(_moba_decode_kernel, n_pages=n_pages),
        out_shape=jax.ShapeDtypeStruct((m, N_HEADS, HEAD_DIM), F32),
        grid_spec=grid_spec,
        compiler_params=_params("parallel"),
    )(page_table, q, kn, vn, *([cache_k] * n_pages), *([cache_v] * n_pages))
    return out.reshape(m, D_MODEL)


def _merge_tail(x, gate, cb, zc, ga, gb, gc, ob, oc, wc_ref, wo_ref):
    out_a = jnp.dot((cb.astype(F32) * zc).astype(BF16), wc_ref[...], preferred_element_type=F32)
    merged = (jax.nn.sigmoid(ga.astype(F32)) * out_a + jax.nn.sigmoid(gb.astype(F32)) * ob.astype(F32)
              + jax.nn.sigmoid(gc.astype(F32)) * oc.astype(F32))
    return x + gate * jnp.dot(merged.astype(BF16), wo_ref[...], preferred_element_type=F32)


def _mix_prompt_kernel(x_ref, gate_ref, cb_ref, cc_ref, cx_ref, ga_ref, gb_ref, gc_ref, ob_ref, oc_ref,
                       cw_ref, wc_ref, wo_ref, xo_ref, last_ref, z_ref, *, tm, tiles_per_seq):
    tail = SUBLANES

    @pl.when(pl.program_id(0) % tiles_per_seq == 0)
    def _():
        z_ref[0:tail, :] = jnp.zeros((tail, D_MODEL), F32)

    z = cc_ref[...].astype(F32) * cx_ref[...].astype(F32)
    z_ref[tail:tail + tm, :] = z
    w = cw_ref[...]
    zc = (z_ref[tail - 2:tail - 2 + tm, :] * w[0:1] + z_ref[tail - 1:tail - 1 + tm, :] * w[1:2] + z * w[2:3])
    z_last = z[tm - tail:, :]
    z_ref[0:tail, :] = z_last
    last_ref[...] = z_last
    xo_ref[...] = _merge_tail(x_ref[...], gate_ref[...], cb_ref[...], zc, ga_ref[...], gb_ref[...], gc_ref[...],
                              ob_ref[...], oc_ref[...], wc_ref, wo_ref)


def _mix_decode_kernel(x_ref, gate_ref, cb_ref, cc_ref, cx_ref, ga_ref, gb_ref, gc_ref, ob_ref, oc_ref,
                       cw_ref, wc_ref, wo_ref, p0_ref, p1_ref, xo_ref, last_ref):
    z = cc_ref[...].astype(F32) * cx_ref[...].astype(F32)
    w = cw_ref[...]
    p1 = p1_ref[...]
    zc = p0_ref[...] * w[0:1] + p1 * w[1:2] + z * w[2:3]
    last_ref[:, 0:D_MODEL] = p1
    last_ref[:, D_MODEL:2 * D_MODEL] = z
    xo_ref[...] = _merge_tail(x_ref[...], gate_ref[...], cb_ref[...], zc, ga_ref[...], gb_ref[...], gc_ref[...],
                              ob_ref[...], oc_ref[...], wc_ref, wo_ref)


def _mix(x, mods, layer, proj, out_b, out_c, conv_w, w_conv_out, w_o, decode, conv_state=None):
    m = x.shape[0]
    tm = m if decode else 512
    gate_arr, gate_spec = _mod_spec(*mods, layer, 2, tm, SEQ, decode)

    def col(c):
        return pl.BlockSpec((tm, D_MODEL), lambda i: (i, c))

    row = pl.BlockSpec((tm, D_MODEL), lambda i: (i, 0))
    wspec = pl.BlockSpec((None, D_MODEL, D_MODEL), lambda i: (layer, 0, 0))
    in_specs = [row, gate_spec, col(COL_CB), col(COL_CC), col(COL_CX), col(COL_GA), col(COL_GB), col(COL_GC),
                row, row, pl.BlockSpec((None, CONV_WIDTH, D_MODEL), lambda i: (layer, 0, 0)), wspec, wspec]
    args = [x, gate_arr, proj, proj, proj, proj, proj, proj, out_b, out_c, conv_w, w_conv_out, w_o]
    if decode:
        state2 = conv_state.reshape(DEPTH, m, (CONV_WIDTH - 1) * D_MODEL)
        in_specs += [pl.BlockSpec((None, m, D_MODEL), lambda i: (layer, 0, 0)),
                     pl.BlockSpec((None, m, D_MODEL), lambda i: (layer, 0, 1))]
        args += [state2, state2]
        return pl.pallas_call(
            _mix_decode_kernel,
            out_shape=(jax.ShapeDtypeStruct((m, D_MODEL), F32),
                       jax.ShapeDtypeStruct((m, (CONV_WIDTH - 1) * D_MODEL), F32)),
            grid=(1,),
            in_specs=in_specs,
            out_specs=(row, pl.BlockSpec((m, (CONV_WIDTH - 1) * D_MODEL), lambda i: (0, 0))),
            compiler_params=_params("arbitrary"),
        )(*args)
    tiles_per_seq = SEQ // tm
    return pl.pallas_call(
        functools.partial(_mix_prompt_kernel, tm=tm, tiles_per_seq=tiles_per_seq),
        out_shape=(jax.ShapeDtypeStruct((m, D_MODEL), F32),
                   jax.ShapeDtypeStruct((BATCH, SUBLANES, D_MODEL), F32)),
        grid=(m // tm,),
        in_specs=in_specs,
        out_specs=(row, pl.BlockSpec((None, SUBLANES, D_MODEL), lambda i: (i // tiles_per_seq, 0, 0))),
        scratch_shapes=[pltpu.VMEM((SUBLANES + tm, D_MODEL), F32)],
        compiler_params=_params("arbitrary"),
    )(*args)


FFN_CHUNKS = (MXU_DIM * (D_FF // MXU_DIM - D_FF // MXU_DIM // 2), MXU_DIM * (D_FF // MXU_DIM // 2))
assert sum(FFN_CHUNKS) == D_FF


def _ffn_kernel(x_ref, g_ref, sc_ref, sh_ref, gate_ref, w1_ref, w3_ref, w2_ref, gf_ref, o_ref, *, final_norm):
    x = x_ref[...]
    h = _mod_norm(x, g_ref[...], sc_ref[...], sh_ref[...]).astype(BF16)
    acc = None
    start = 0
    for width in FFN_CHUNKS:
        cols = slice(start, start + width)
        start += width
        a = (jax.nn.silu(jnp.dot(h, w1_ref[:, cols], preferred_element_type=F32))
             * jnp.dot(h, w3_ref[:, cols], preferred_element_type=F32))
        part = jnp.dot(a.astype(BF16), w2_ref[cols, :], preferred_element_type=F32)
        acc = part if acc is None else acc + part
    y = x + gate_ref[...] * acc
    if final_norm:
        y = y * lax.rsqrt(jnp.mean(y * y, axis=-1, keepdims=True) + EPS) * gf_ref[...]
    o_ref[...] = y


def _ffn(x, mods, layer, g_ffn, w1, w3, w2, g_final, decode):
    m = x.shape[0]
    tm = m if decode else 512
    sc_arr, sc_spec = _mod_spec(*mods, layer, 4, tm, SEQ, decode)
    sh_arr, sh_spec = _mod_spec(*mods, layer, 3, tm, SEQ, decode)
    gt_arr, gt_spec = _mod_spec(*mods, layer, 5, tm, SEQ, decode)
    row = pl.BlockSpec((tm, D_MODEL), lambda i: (i, 0))

    def resident(shape):
        return pl.BlockSpec((None,) + shape, lambda i: (layer, 0, 0), pipeline_mode=pl.Buffered(1))

    return pl.pallas_call(
        functools.partial(_ffn_kernel, final_norm=layer == DEPTH - 1),
        out_shape=jax.ShapeDtypeStruct((m, D_MODEL), F32),
        grid=(m // tm,),
        in_specs=[row, pl.BlockSpec((None, 1, D_MODEL), lambda i: (layer, 0, 0)), sc_spec, sh_spec, gt_spec,
                  resident((D_MODEL, D_FF)), resident((D_MODEL, D_FF)), resident((D_FF, D_MODEL)),
                  pl.BlockSpec((1, D_MODEL), lambda i: (0, 0))],
        out_specs=row,
        compiler_params=_params("parallel"),
    )(x, g_ffn, sc_arr, sh_arr, gt_arr, w1, w3, w2, g_final.reshape(1, D_MODEL))


def _trunk(x, mods, w, decode, state=None):
    ks, vs, convs, res, ims = [], [], [], [], []
    kv_stacks = None
    for l in range(DEPTH):
        mix, qkv = _in_proj(x, mods, l, w["g_mix"], w["w_in"], decode)
        if decode:
            out_b, s_re, s_im = _s5_decode(mix, state["ssm_re"], state["ssm_im"], l, w["s5"])
            out_c = _moba_decode(qkv, state["cache_k"], state["cache_v"], state["page_table"], l)
            ks.append(qkv[:, COL_K:COL_K + KV_DIM])
            vs.append(qkv[:, COL_V:COL_V + KV_DIM])
            x, conv_last = _mix(x, mods, l, mix, out_b, out_c, w["conv_w"], w["w_conv_out"], w["w_o"], True,
                                state["conv"])
            conv_last = conv_last.reshape(DEC_BATCH, CONV_WIDTH - 1, D_MODEL)
        else:
            out_b, s_re, s_im = _s5_prompt(mix.reshape(BATCH, SEQ, D_MIX), l, w["s5"])
            out_c, kv_stacks = _moba_prompt(qkv.reshape(BATCH, SEQ, D_QKV), l, kv_stacks)
            x, conv_last = _mix(x, mods, l, mix, out_b.reshape(BATCH * SEQ, D_MODEL),
                                out_c.reshape(BATCH * SEQ, D_MODEL), w["conv_w"], w["w_conv_out"], w["w_o"], False)
            conv_last = conv_last[:, SUBLANES - (CONV_WIDTH - 1):, :]
        x = _ffn(x, mods, l, w["g_ffn"], w["w_ff1"], w["w_ff3"], w["w_ff2"], w["g_final"], decode)
        convs.append(conv_last)
        res.append(s_re)
        ims.append(s_im)
    y = x
    k_all, v_all = (jnp.stack(ks), jnp.stack(vs)) if decode else kv_stacks
    return y, k_all, v_all, jnp.stack(convs), jnp.stack(res), jnp.stack(ims)


def kernel(x_prompt, x_sample, cache_k, cache_v, state_conv, state_ssm_re, state_ssm_im, page_table, c_prompt, c_sample, w_ada, b_ada, g_mix, w_in, conv_w, w_conv_out, ssm_lam_re, ssm_lam_im, ssm_log_step, ssm_b_re, ssm_b_im, ssm_c_re, ssm_c_im, ssm_d, w_glu_a, w_glu_b, w_o, g_ffn, w_ff1, w_ff3, w_ff2, g_final):
    n_pool = cache_k.shape[1]
    mod = _ada(jnp.concatenate([c_sample, c_prompt], axis=0), w_ada, b_ada)
    mods = (mod, mod.reshape(DEPTH, DEC_BATCH + BATCH, 6, 1, D_MODEL))

    lam_re, lam_im, bb_re, bb_im = _s5_discretize(
        ssm_lam_re, ssm_lam_im, ssm_log_step, jnp.swapaxes(ssm_b_re, 2, 3), jnp.swapaxes(ssm_b_im, 2, 3))
    s5w = (lam_re.reshape(DEPTH, 1, SSM_LANES), lam_im.reshape(DEPTH, 1, SSM_LANES),
           _block_diag_in(bb_re).astype(BF16), _block_diag_in(bb_im).astype(BF16),
           _block_diag_out(ssm_c_re).astype(BF16), _block_diag_out(ssm_c_im).astype(BF16),
           ssm_d.reshape(DEPTH, 1, D_MODEL), w_glu_a.astype(BF16), w_glu_b.astype(BF16))
    w = dict(g_mix=g_mix.reshape(DEPTH, 1, D_MODEL), w_in=w_in.astype(BF16), s5=s5w, conv_w=conv_w,
             w_conv_out=w_conv_out.astype(BF16), w_o=w_o.astype(BF16), g_ffn=g_ffn.reshape(DEPTH, 1, D_MODEL),
             w_ff1=w_ff1.astype(BF16), w_ff3=w_ff3.astype(BF16), w_ff2=w_ff2.astype(BF16), g_final=g_final)

    yp, kp, vp, cp, rp, ip = _trunk(x_prompt.reshape(BATCH * SEQ, D_MODEL), mods, w, False)
    state = dict(ssm_re=state_ssm_re.reshape(DEPTH, DEC_BATCH, SSM_LANES),
                 ssm_im=state_ssm_im.reshape(DEPTH, DEC_BATCH, SSM_LANES),
                 cache_k=cache_k.reshape(DEPTH, n_pool, PAGE_ROWS, HEAD_DIM),
                 cache_v=cache_v.reshape(DEPTH, n_pool, PAGE_ROWS, HEAD_DIM),
                 page_table=page_table, conv=state_conv)
    ys, ksm, vsm, cs, rs, is_ = _trunk(x_sample.reshape(DEC_BATCH, D_MODEL), mods, w, True, state)

    kv_p = (DEPTH, BATCH, SEQ, N_KV_HEADS, HEAD_DIM)
    kv_s = (DEPTH, DEC_BATCH, 1, N_KV_HEADS, HEAD_DIM)
    st_p = (DEPTH, BATCH, SSM_GROUPS, SSM_STATE)
    st_s = (DEPTH, DEC_BATCH, SSM_GROUPS, SSM_STATE)
    return (yp.reshape(BATCH, SEQ, D_MODEL), ys.reshape(DEC_BATCH, 1, D_MODEL),
            kp.reshape(kv_p), vp.reshape(kv_p), cp, rp.reshape(st_p), ip.reshape(st_p),
            ksm.reshape(kv_s), vsm.reshape(kv_s), cs, rs.reshape(st_s), is_.reshape(st_s))
```

```python
import functools

import jax
import jax.numpy as jnp
from jax import lax
from jax.experimental import pallas as pl
from jax.experimental.pallas import tpu as pltpu

F32 = jnp.float32
BF16 = jnp.bfloat16

D_MODEL = 1024
DEPTH = 4
SEQ = 2048
BATCH = 8
DEC_BATCH = 128
PAGE_SIZE = 128
EPS = 1e-6
CONV_WIDTH = 3
SSM_GROUP = 16
SSM_GROUPS = D_MODEL // SSM_GROUP
SSM_STATE = 64
SSM_LANES = SSM_GROUPS * SSM_STATE
HEAD_DIM = 128
N_HEADS = D_MODEL // HEAD_DIM
N_KV_HEADS = N_HEADS // 2
KV_GROUP = N_HEADS // N_KV_HEADS
KV_DIM = N_KV_HEADS * HEAD_DIM
KV_GROUP_SHIFT = KV_GROUP.bit_length() - 1
assert KV_GROUP == 1 << KV_GROUP_SHIFT and N_KV_HEADS & (N_KV_HEADS - 1) == 0
PAGE_ROWS = PAGE_SIZE * N_KV_HEADS
MOBA_BLOCK = 256
MOBA_TOPK = 3
ATTN_SCALE = HEAD_DIM ** -0.5
NEG = -1e30
D_FF = 2816
D_IN = 9216
COL_CB, COL_CC, COL_CX, COL_U, COL_GA, COL_GB, COL_GC = 0, 1, 2, 3, 4, 5, 6
D_MIX = 7 * D_MODEL
COL_Q = 0
COL_K = D_MODEL
COL_V = COL_K + KV_DIM
D_QKV = D_MODEL + 2 * KV_DIM
W_QKV = 4 * D_MODEL

LANES = 128
SUBLANES = 8
MXU_DIM = 256
VMEM_LIMIT = 56 * 1024 * 1024

S5_SLABS = D_MODEL // MXU_DIM
S5_SLAB_LANES = SSM_LANES // S5_SLABS


def _params(*sem):
    return pltpu.CompilerParams(dimension_semantics=sem, vmem_limit_bytes=VMEM_LIMIT)


def _ada_kernel(c_ref, w_ref, b_ref, o_ref):
    c = jax.nn.silu(c_ref[...]).astype(BF16)
    o_ref[...] = jnp.dot(c, w_ref[...].astype(BF16), preferred_element_type=F32) + b_ref[...]


def _ada(c_all, w_ada, b_ada):
    n_rows = c_all.shape[0]
    tn = 1536
    return pl.pallas_call(
        _ada_kernel,
        out_shape=jax.ShapeDtypeStruct((DEPTH, n_rows, 6 * D_MODEL), F32),
        grid=(DEPTH, 6 * D_MODEL // tn),
        in_specs=[
            pl.BlockSpec((n_rows, D_MODEL), lambda l, j: (0, 0)),
            pl.BlockSpec((None, D_MODEL, tn), lambda l, j: (l, 0, j)),
            pl.BlockSpec((None, 1, tn), lambda l, j: (l, 0, j)),
        ],
        out_specs=pl.BlockSpec((None, n_rows, tn), lambda l, j: (l, 0, j)),
        compiler_params=_params("parallel", "parallel"),
    )(c_all, w_ada, b_ada.reshape(DEPTH, 1, 6 * D_MODEL))


def _s5_disc_kernel(lre_ref, lim_ref, ls_ref, bre_ref, bim_ref, lbre_ref, lbim_ref, bbre_ref, bbim_ref):
    lre = lre_ref[...]
    lim = lim_ref[...]
    step = jnp.exp(ls_ref[...])
    mag = jnp.exp(lre * step)
    ang = lim * step
    lbre = mag * jnp.cos(ang)
    lbim = mag * jnp.sin(ang)
    lbre_ref[...] = lbre
    lbim_ref[...] = lbim
    nre = lbre - 1.0
    den = lre * lre + lim * lim
    cre = (nre * lre + lbim * lim) / den
    cim = (lbim * lre - nre * lim) / den
    bre = bre_ref[...]
    bim = bim_ref[...]
    bbre_ref[...] = cre * bre - cim * bim
    bbim_ref[...] = cre * bim + cim * bre


def _s5_discretize(lam_re, lam_im, log_step, b_re_t, b_im_t):
    g, p, i = SSM_GROUPS, SSM_STATE, SSM_GROUP
    gp = pl.BlockSpec((None, g, 1, p), lambda l: (l, 0, 0, 0))
    gip = pl.BlockSpec((None, g, i, p), lambda l: (l, 0, 0, 0))
    return pl.pallas_call(
        _s5_disc_kernel,
        out_shape=(jax.ShapeDtypeStruct((DEPTH, g, 1, p), F32), jax.ShapeDtypeStruct((DEPTH, g, 1, p), F32),
                   jax.ShapeDtypeStruct((DEPTH, g, i, p), F32), jax.ShapeDtypeStruct((DEPTH, g, i, p), F32)),
        grid=(DEPTH,),
        in_specs=[gp, gp, pl.BlockSpec((None, g, 1, 1), lambda l: (l, 0, 0, 0)), gip, gip],
        out_specs=(gp, gp, gip, gip),
        compiler_params=_params("parallel"),
    )(lam_re.reshape(DEPTH, g, 1, p), lam_im.reshape(DEPTH, g, 1, p), log_step.reshape(DEPTH, g, 1, 1),
      b_re_t, b_im_t)


def _block_diag_in(b_t):
    return _tile_diag(b_t.reshape(DEPTH, S5_SLABS, MXU_DIM, SSM_STATE), SSM_GROUP, SSM_STATE)


def _block_diag_out(c):
    ct = jnp.swapaxes(c, 2, 3).reshape(DEPTH, S5_SLABS, S5_SLAB_LANES, SSM_GROUP)
    return _tile_diag(ct, SSM_STATE, SSM_GROUP)


def _tile_diag(a, rows_per_group, cols_per_group):
    gs = SSM_GROUPS // S5_SLABS
    tiled = jnp.tile(a, (1, 1, 1, gs))
    row_g = jnp.arange(gs * rows_per_group)[:, None] // rows_per_group
    col_g = jnp.arange(gs * cols_per_group)[None, :] // cols_per_group
    return jnp.where(row_g == col_g, tiled, 0.0)


def _mod_norm(x, g, scale, shift):
    y = x * lax.rsqrt(jnp.mean(x * x, axis=-1, keepdims=True) + EPS)
    return (y * g) * (1.0 + scale) + shift


IN_PROJ_CHUNK = 1024


def _in_proj_kernel(x_ref, g_ref, sc_ref, sh_ref, w_ref, mix_ref, qkv_ref):
    h = _mod_norm(x_ref[...], g_ref[...], sc_ref[...], sh_ref[...]).astype(BF16)
    nc = IN_PROJ_CHUNK
    for n in range(D_MIX // nc):
        src = n * nc if n * nc < W_QKV else n * nc + D_QKV
        mix_ref[:, n * nc:(n + 1) * nc] = jnp.dot(
            h, w_ref[:, src:src + nc], preferred_element_type=F32).astype(BF16)
    for n in range(D_QKV // nc):
        src = W_QKV + n * nc
        qkv_ref[:, n * nc:(n + 1) * nc] = jnp.dot(h, w_ref[:, src:src + nc], preferred_element_type=F32)


def _mod_spec(mod, mod5, layer, chunk, tm, rows_per_seq, decode):
    if decode:
        return mod, pl.BlockSpec((None, tm, D_MODEL), lambda i, *_: (layer, 0, chunk))
    return mod5, pl.BlockSpec((None, None, None, 1, D_MODEL),
                              lambda i, *_: (layer, DEC_BATCH + (i * tm) // rows_per_seq, chunk, 0, 0))


def _in_proj(x, mods, layer, g_mix, w_in, decode):
    m = x.shape[0]
    tm = m if decode else 512
    sc_arr, sc_spec = _mod_spec(*mods, layer, 1, tm, SEQ, decode)
    sh_arr, sh_spec = _mod_spec(*mods, layer, 0, tm, SEQ, decode)
    return pl.pallas_call(
        _in_proj_kernel,
        out_shape=(jax.ShapeDtypeStruct((m, D_MIX), BF16), jax.ShapeDtypeStruct((m, D_QKV), F32)),
        grid=(m // tm,),
        in_specs=[
            pl.BlockSpec((tm, D_MODEL), lambda i: (i, 0)),
            pl.BlockSpec((None, 1, D_MODEL), lambda i: (layer, 0, 0)),
            sc_spec, sh_spec,
            pl.BlockSpec((None, D_MODEL, D_IN), lambda i: (layer, 0, 0), pipeline_mode=pl.Buffered(1)),
        ],
        out_specs=(pl.BlockSpec((tm, D_MIX), lambda i: (i, 0)), pl.BlockSpec((tm, D_QKV), lambda i: (i, 0))),
        compiler_params=_params("parallel"),
    )(x, g_mix, sc_arr, sh_arr, w_in)


def _s5_readout(u_slab, sre_slab, sim_slab, cre_ref, cim_ref, d_ref, s):
    y = (jnp.dot(sre_slab.astype(BF16), cre_ref[s], preferred_element_type=F32)
         - jnp.dot(sim_slab.astype(BF16), cim_ref[s], preferred_element_type=F32))
    y = y + u_slab * d_ref[:, s * MXU_DIM:(s + 1) * MXU_DIM]
    return jax.nn.gelu(y, approximate=True)


def _glu(g, wa_ref, wb_ref):
    gb = g.astype(BF16)
    return (jnp.dot(gb, wa_ref[...], preferred_element_type=F32)
            * jax.nn.sigmoid(jnp.dot(gb, wb_ref[...], preferred_element_type=F32)))


SCAN_LANES = 512


def _s5_prompt_kernel(u_ref, lre_ref, lim_ref, bre_ref, bim_ref, cre_ref, cim_ref, d_ref, wa_ref, wb_ref,
                      ob_ref, fre_ref, fim_ref,
                      us_ref, sre_ref, sim_ref, os_ref, st_ref, *, tt):
    n_lane_slabs = D_MODEL // LANES

    @pl.when(pl.program_id(0) == 0)
    def _():
        st_ref[...] = jnp.zeros_like(st_ref)

    for b in range(BATCH):
        for sl in range(n_lane_slabs):
            us_ref[sl, pl.ds(b, tt, stride=BATCH), :] = u_ref[b, :, sl * LANES:(sl + 1) * LANES].astype(F32)

    def u_slab(s):
        per = MXU_DIM // LANES
        return jnp.concatenate([us_ref[per * s + k] for k in range(per)], axis=1)

    gs = []
    for s in range(S5_SLABS):
        ub = u_slab(s).astype(BF16)
        slab = slice(s * S5_SLAB_LANES, (s + 1) * S5_SLAB_LANES)
        sre_ref[:, slab] = jnp.dot(ub, bre_ref[s], preferred_element_type=F32)
        sim_ref[:, slab] = jnp.dot(ub, bim_ref[s], preferred_element_type=F32)
        for c in range(S5_SLAB_LANES // SCAN_LANES):
            cols = slice(s * S5_SLAB_LANES + c * SCAN_LANES, s * S5_SLAB_LANES + (c + 1) * SCAN_LANES)
            lre = jnp.broadcast_to(lre_ref[:, cols], (BATCH, SCAN_LANES))
            lim = jnp.broadcast_to(lim_ref[:, cols], (BATCH, SCAN_LANES))
            pre, pim = st_ref[0, :, cols], st_ref[1, :, cols]
            for t in range(tt):
                rows = slice(t * BATCH, (t + 1) * BATCH)
                pre, pim = (lre * pre - lim * pim + sre_ref[rows, cols],
                            lre * pim + lim * pre + sim_ref[rows, cols])
                sre_ref[rows, cols] = pre
                sim_ref[rows, cols] = pim
            st_ref[0, :, cols] = pre
            st_ref[1, :, cols] = pim
        gs.append(_s5_readout(u_slab(s), sre_ref[:, slab], sim_ref[:, slab], cre_ref, cim_ref, d_ref, s))

    fre_ref[...] = st_ref[0]
    fim_ref[...] = st_ref[1]

    ob = _glu(jnp.concatenate(gs, axis=1), wa_ref, wb_ref)

    for sl in range(n_lane_slabs):
        os_ref[sl] = ob[:, sl * LANES:(sl + 1) * LANES]
    for b in range(BATCH):
        for sl in range(n_lane_slabs):
            ob_ref[b, :, sl * LANES:(sl + 1) * LANES] = os_ref[sl, pl.ds(b, tt, stride=BATCH), :].astype(BF16)


def _s5_decode_kernel(u_ref, s0re_ref, s0im_ref, lre_ref, lim_ref, bre_ref, bim_ref, cre_ref, cim_ref, d_ref,
                      wa_ref, wb_ref, ob_ref, fre_ref, fim_ref):
    gs = []
    for s in range(S5_SLABS):
        cols = slice(s * S5_SLAB_LANES, (s + 1) * S5_SLAB_LANES)
        ub = u_ref[:, s * MXU_DIM:(s + 1) * MXU_DIM]
        u = ub.astype(F32)
        lre, lim = lre_ref[:, cols], lim_ref[:, cols]
        pre, pim = s0re_ref[:, cols], s0im_ref[:, cols]
        nre = lre * pre - lim * pim + jnp.dot(ub, bre_ref[s], preferred_element_type=F32)
        nim = lre * pim + lim * pre + jnp.dot(ub, bim_ref[s], preferred_element_type=F32)
        fre_ref[:, cols] = nre
        fim_ref[:, cols] = nim
        gs.append(_s5_readout(u, nre, nim, cre_ref, cim_ref, d_ref, s))
    ob_ref[...] = _glu(jnp.concatenate(gs, axis=1), wa_ref, wb_ref)


def _s5_weight_specs(layer, n_grid):
    def const(shape, *idx):
        return pl.BlockSpec(shape, lambda *_: idx, pipeline_mode=pl.Buffered(1))

    return [
        const((None, 1, SSM_LANES), layer, 0, 0),
        const((None, 1, SSM_LANES), layer, 0, 0),
        const((None, S5_SLABS, MXU_DIM, S5_SLAB_LANES), layer, 0, 0, 0),
        const((None, S5_SLABS, MXU_DIM, S5_SLAB_LANES), layer, 0, 0, 0),
        const((None, S5_SLABS, S5_SLAB_LANES, MXU_DIM), layer, 0, 0, 0),
        const((None, S5_SLABS, S5_SLAB_LANES, MXU_DIM), layer, 0, 0, 0),
        const((None, 1, D_MODEL), layer, 0, 0),
        const((None, D_MODEL, D_MODEL), layer, 0, 0),
        const((None, D_MODEL, D_MODEL), layer, 0, 0),
    ]


def _s5_prompt(mix3, layer, s5w):
    tt = 64
    rows = tt * BATCH
    kern = functools.partial(_s5_prompt_kernel, tt=tt)
    state = jax.ShapeDtypeStruct((BATCH, SSM_LANES), F32)
    state_spec = pl.BlockSpec((BATCH, SSM_LANES), lambda i: (0, 0))
    return pl.pallas_call(
        kern,
        out_shape=(jax.ShapeDtypeStruct((BATCH, SEQ, D_MODEL), BF16), state, state),
        grid=(SEQ // tt,),
        in_specs=[pl.BlockSpec((BATCH, tt, D_MODEL), lambda i: (0, i, COL_U))] + _s5_weight_specs(layer, 1),
        out_specs=(pl.BlockSpec((BATCH, tt, D_MODEL), lambda i: (0, i, 0)), state_spec, state_spec),
        scratch_shapes=[
            pltpu.VMEM((D_MODEL // LANES, rows, LANES), F32),
            pltpu.VMEM((rows, SSM_LANES), F32),
            pltpu.VMEM((rows, SSM_LANES), F32),
            pltpu.VMEM((D_MODEL // LANES, rows, LANES), F32),
            pltpu.VMEM((2, BATCH, SSM_LANES), F32),
        ],
        compiler_params=_params("arbitrary"),
    )(mix3, *s5w)


def _s5_decode(mix, s0re, s0im, layer, s5w):
    m = mix.shape[0]
    state = jax.ShapeDtypeStruct((m, SSM_LANES), F32)
    full = pl.BlockSpec((m, SSM_LANES), lambda i: (0, 0))
    return pl.pallas_call(
        _s5_decode_kernel,
        out_shape=(jax.ShapeDtypeStruct((m, D_MODEL), F32), state, state),
        grid=(1,),
        in_specs=[pl.BlockSpec((m, D_MODEL), lambda i: (0, COL_U)),
                  pl.BlockSpec((None, m, SSM_LANES), lambda i: (layer, 0, 0)),
                  pl.BlockSpec((None, m, SSM_LANES), lambda i: (layer, 0, 0))] + _s5_weight_specs(layer, 1),
        out_specs=(pl.BlockSpec((m, D_MODEL), lambda i: (0, 0)), full, full),
        compiler_params=_params("arbitrary"),
    )(mix, s0re, s0im, *s5w)


def _topk_select(gate, valid, n_candidates):
    lane = lax.broadcasted_iota(jnp.int32, gate.shape, 1)
    g = jnp.where(valid, gate, NEG)
    rank = jnp.zeros(gate.shape, jnp.int32)
    for jp in range(n_candidates):
        col = g[:, jp:jp + 1]
        rank = rank + jnp.where(col > g, 1, jnp.where(col == g, jnp.where(lane > jp, 1, 0), 0))
    return jnp.where(valid, jnp.where(rank < MOBA_TOPK, 1.0, 0.0), 0.0)


def _moba_prompt_kernel(q_ref, k_ref, v_ref, *refs):
    o_ref, ko_ref, vo_ref = refs[-3:]
    nb = SEQ // MOBA_BLOCK
    k = k_ref[...]
    v = v_ref[...]
    head_rows = pl.ds(pl.program_id(1), SEQ, stride=N_KV_HEADS)
    ko_ref[head_rows, :] = k
    vo_ref[head_rows, :] = v
    assert nb <= SUBLANES
    vb = jnp.concatenate([v.astype(BF16), jnp.ones((SEQ, HEAD_DIM), BF16)], axis=1)
    key_blk = lax.broadcasted_iota(jnp.int32, (SEQ, LANES), 0) // MOBA_BLOCK
    onehot = jnp.where(key_blk == lax.broadcasted_iota(jnp.int32, (SEQ, LANES), 1), 1.0, 0.0)
    kb = jnp.concatenate([k.astype(BF16), onehot.astype(BF16)], axis=1)
    kmean = jnp.mean(k.reshape(nb, MOBA_BLOCK, HEAD_DIM), axis=1).astype(BF16)
    rows = KV_GROUP * MOBA_BLOCK
    blk = lax.broadcasted_iota(jnp.int32, (nb, rows), 0)
    q_in_blk = lax.broadcasted_iota(jnp.int32, (rows, MOBA_BLOCK), 0) & (MOBA_BLOCK - 1)
    k_in_blk = lax.broadcasted_iota(jnp.int32, (rows, MOBA_BLOCK), 1)
    causal = k_in_blk <= q_in_blk
    nt = (((1,), (1,)), ((), ()))
    for i in range(nb):
        q2 = q_ref[i * MOBA_BLOCK:(i + 1) * MOBA_BLOCK, :]
        q = jnp.concatenate([q2[:, h * HEAD_DIM:(h + 1) * HEAD_DIM] for h in range(KV_GROUP)], axis=0)
        nk = (i + 1) * MOBA_BLOCK
        if i > 0:
            gate = lax.dot_general(kmean, q.astype(BF16), nt, preferred_element_type=F32)
            g = jnp.where(blk < i, gate, NEG)
            rank = jnp.zeros((nb, rows), jnp.int32)
            for jp in range(i):
                other = g[jp:jp + 1, :]
                rank = rank + jnp.where(other > g, 1, jnp.where(other == g, jnp.where(blk > jp, 1, 0), 0))
            bias = jnp.where(blk < i, jnp.where(rank < MOBA_TOPK, 0.0, NEG), 0.0)
            bias = jnp.concatenate([bias, jnp.zeros((LANES - nb, rows), F32)], axis=0).T
        else:
            bias = jnp.zeros((rows, LANES), F32)
        qa = jnp.concatenate([(q * ATTN_SCALE).astype(BF16), bias.astype(BF16)], axis=1)
        s = lax.dot_general(qa, kb[:nk], nt, preferred_element_type=F32)
        diag = jnp.where(causal, s[:, i * MOBA_BLOCK:], NEG)
        s = jnp.concatenate([s[:, :i * MOBA_BLOCK], diag], axis=1) if i > 0 else diag
        m = jnp.max(s, axis=-1, keepdims=True)
        p = jnp.exp(s - m)
        o = jnp.dot(p.astype(BF16), vb[:nk], preferred_element_type=F32)
        o = o[:, :HEAD_DIM] / o[:, HEAD_DIM:]
        for h in range(KV_GROUP):
            o_ref[i * MOBA_BLOCK:(i + 1) * MOBA_BLOCK, h * HEAD_DIM:(h + 1) * HEAD_DIM] = (
                o[h * MOBA_BLOCK:(h + 1) * MOBA_BLOCK].astype(BF16))


def _moba_prompt(proj3, layer, kv_stacks):
    gw = KV_GROUP * HEAD_DIM
    kv = jax.ShapeDtypeStruct((DEPTH, BATCH, SEQ * N_KV_HEADS, HEAD_DIM), F32)
    kv_spec = pl.BlockSpec((None, None, SEQ * N_KV_HEADS, HEAD_DIM), lambda b, h: (layer, b, 0, 0))
    in_specs = [
        pl.BlockSpec((None, SEQ, gw), lambda b, h: (b, 0, COL_Q // gw + h)),
        pl.BlockSpec((None, SEQ, HEAD_DIM), lambda b, h: (b, 0, COL_K // HEAD_DIM + h)),
        pl.BlockSpec((None, SEQ, HEAD_DIM), lambda b, h: (b, 0, COL_V // HEAD_DIM + h)),
    ]
    args = [proj3, proj3, proj3]
    aliases = {}
    if kv_stacks is not None:
        in_specs += [pl.BlockSpec(memory_space=pl.ANY)] * 2
        args += list(kv_stacks)
        aliases = {3: 1, 4: 2}
    out_c, k_stack, v_stack = pl.pallas_call(
        _moba_prompt_kernel,
        out_shape=(jax.ShapeDtypeStruct((BATCH, SEQ, D_MODEL), BF16), kv, kv),
        grid=(BATCH, N_KV_HEADS),
        in_specs=in_specs,
        out_specs=(pl.BlockSpec((None, SEQ, gw), lambda b, h: (b, 0, h)), kv_spec, kv_spec),
        input_output_aliases=aliases,
        compiler_params=_params("parallel", "arbitrary"),
    )(*args)
    return out_c, (k_stack, v_stack)


def _moba_decode_kernel(pt_ref, q_ref, kn_ref, vn_ref, *refs, n_pages):
    del pt_ref
    assert SUBLANES == 2 * N_KV_HEADS
    k_pages = refs[:n_pages]
    v_pages = refs[n_pages:2 * n_pages]
    o_ref = refs[2 * n_pages]
    per = MOBA_BLOCK // PAGE_SIZE
    nb = n_pages // per
    nt = (((1,), (1,)), ((), ()))
    q = q_ref[...]
    qs = (q * ATTN_SCALE).astype(BF16)
    head_kv = lax.broadcasted_iota(jnp.int32, (N_HEADS, PAGE_ROWS), 0) >> KV_GROUP_SHIFT
    row_kv = lax.broadcasted_iota(jnp.int32, (N_HEADS, PAGE_ROWS), 1) & (N_KV_HEADS - 1)
    own_kv = row_kv == head_kv
    scores, ksums = [], []
    for pg in range(n_pages):
        kp = k_pages[pg][...]
        scores.append(lax.dot_general(qs, kp.astype(BF16), nt, preferred_element_type=F32))
        ksums.append(jnp.sum(kp.reshape(PAGE_ROWS // SUBLANES, SUBLANES, HEAD_DIM), axis=0))
    kmeans = []
    for j in range(nb):
        ks = ksums[per * j]
        for r in range(1, per):
            ks = ks + ksums[per * j + r]
        ks = ks + pltpu.roll(ks, N_KV_HEADS, 0)
        kmeans.append(ks * (1.0 / MOBA_BLOCK))
    kmean = jnp.concatenate(kmeans, axis=0).astype(BF16)
    gate = lax.dot_general(q.astype(BF16), kmean, nt, preferred_element_type=F32)
    n_cand = nb * SUBLANES
    cand = lax.broadcasted_iota(jnp.int32, (N_HEADS, n_cand), 1) & (SUBLANES - 1)
    cand_head_kv = lax.broadcasted_iota(jnp.int32, (N_HEADS, n_cand), 0) >> KV_GROUP_SHIFT
    sel = _topk_select(gate, cand == cand_head_kv, n_cand)
    masked = []
    for j in range(nb):
        on = jnp.max(sel[:, j * SUBLANES:(j + 1) * SUBLANES], axis=-1, keepdims=True) > 0.0
        for r in range(per):
            masked.append(jnp.where(own_kv, jnp.where(on, scores[per * j + r], NEG), NEG))
    kn = kn_ref[...]
    vn = vn_ref[...]
    kn_h = jnp.concatenate([kn[h // KV_GROUP:h // KV_GROUP + 1] for h in range(N_HEADS)], axis=0)
    vn_h = jnp.concatenate([vn[h // KV_GROUP:h // KV_GROUP + 1] for h in range(N_HEADS)], axis=0)
    s_own = jnp.sum(q * ATTN_SCALE * kn_h, axis=-1, keepdims=True)
    m = jnp.maximum(jnp.max(functools.reduce(jnp.maximum, masked), axis=-1, keepdims=True), s_own)
    p_own = jnp.exp(s_own - m)
    acc = p_own * vn_h
    p_sum = jnp.zeros((N_HEADS, PAGE_ROWS), F32)
    for pg in range(n_pages):
        p = jnp.exp(masked[pg] - m)
        p_sum = p_sum + p
        acc = acc + jnp.dot(p.astype(BF16), v_pages[pg][...].astype(BF16), preferred_element_type=F32)
    l = jnp.sum(p_sum, axis=-1, keepdims=True) + p_own
    o_ref[...] = acc / l


def _moba_decode(proj, cache_k, cache_v, page_table, layer):
    m = proj.shape[0]
    n_pages = page_table.shape[1]
    q = proj[:, COL_Q:COL_Q + D_MODEL].reshape(m, N_HEADS, HEAD_DIM)
    kn = proj[:, COL_K:COL_K + KV_DIM].reshape(m, N_KV_HEADS, HEAD_DIM)
    vn = proj[:, COL_V:COL_V + KV_DIM].reshape(m, N_KV_HEADS, HEAD_DIM)

    def page_spec(pg):
        return pl.BlockSpec((None, None, PAGE_ROWS, HEAD_DIM), lambda b, pt: (layer, pt[b, pg], 0, 0))

    grid_spec = pltpu.PrefetchScalarGridSpec(
        num_scalar_prefetch=1,
        grid=(m,),
        in_specs=[
            pl.BlockSpec((None, N_HEADS, HEAD_DIM), lambda b, pt: (b, 0, 0)),
            pl.BlockSpec((None, N_KV_HEADS, HEAD_DIM), lambda b, pt: (b, 0, 0)),
            pl.BlockSpec((None, N_KV_HEADS, HEAD_DIM), lambda b, pt: (b, 0, 0)),
        ] + [page_spec(pg) for pg in range(n_pages)] + [page_spec(pg) for pg in range(n_pages)],
        out_specs=pl.BlockSpec((None, N_HEADS, HEAD_DIM), lambda b, pt: (b, 0, 0)),
    )
    out = pl.pallas_call(
        functools.partial(_moba_decode_kernel, n_pages=n_pages),
        out_shape=jax.ShapeDtypeStruct((m, N_HEADS, HEAD_DIM), F32),
        grid_spec=grid_spec,
        compiler_params=_params("parallel"),
    )(page_table, q, kn, vn, *([cache_k] * n_pages), *([cache_v] * n_pages))
    return out.reshape(m, D_MODEL)


def _merge_tail(x, gate, cb, zc, ga, gb, gc, ob, oc, wc_ref, wo_ref):
    out_a = jnp.dot((cb.astype(F32) * zc).astype(BF16), wc_ref[...], preferred_element_type=F32)
    merged = (jax.nn.sigmoid(ga.astype(F32)) * out_a + jax.nn.sigmoid(gb.astype(F32)) * ob.astype(F32)
              + jax.nn.sigmoid(gc.astype(F32)) * oc.astype(F32))
    return x + gate * jnp.dot(merged.astype(BF16), wo_ref[...], preferred_element_type=F32)


def _mix_prompt_kernel(x_ref, gate_ref, cb_ref, cc_ref, cx_ref, ga_ref, gb_ref, gc_ref, ob_ref, oc_ref,
                       cw_ref, wc_ref, wo_ref, xo_ref, last_ref, z_ref, *, tm, tiles_per_seq):
    tail = SUBLANES

    @pl.when(pl.program_id(0) % tiles_per_seq == 0)
    def _():
        z_ref[0:tail, :] = jnp.zeros((tail, D_MODEL), F32)

    z = cc_ref[...].astype(F32) * cx_ref[...].astype(F32)
    z_ref[tail:tail + tm, :] = z
    w = cw_ref[...]
    zc = (z_ref[tail - 2:tail - 2 + tm, :] * w[0:1] + z_ref[tail - 1:tail - 1 + tm, :] * w[1:2] + z * w[2:3])
    z_last = z[tm - tail:, :]
    z_ref[0:tail, :] = z_last
    last_ref[...] = z_last
    xo_ref[...] = _merge_tail(x_ref[...], gate_ref[...], cb_ref[...], zc, ga_ref[...], gb_ref[...], gc_ref[...],
                              ob_ref[...], oc_ref[...], wc_ref, wo_ref)


def _mix_decode_kernel(x_ref, gate_ref, cb_ref, cc_ref, cx_ref, ga_ref, gb_ref, gc_ref, ob_ref, oc_ref,
                       cw_ref, wc_ref, wo_ref, p0_ref, p1_ref, xo_ref, last_ref):
    z = cc_ref[...].astype(F32) * cx_ref[...].astype(F32)
    w = cw_ref[...]
    p1 = p1_ref[...]
    zc = p0_ref[...] * w[0:1] + p1 * w[1:2] + z * w[2:3]
    last_ref[:, 0:D_MODEL] = p1
    last_ref[:, D_MODEL:2 * D_MODEL] = z
    xo_ref[...] = _merge_tail(x_ref[...], gate_ref[...], cb_ref[...], zc, ga_ref[...], gb_ref[...], gc_ref[...],
                              ob_ref[...], oc_ref[...], wc_ref, wo_ref)


def _mix(x, mods, layer, proj, out_b, out_c, conv_w, w_conv_out, w_o, decode, conv_state=None):
    m = x.shape[0]
    tm = m if decode else 512
    gate_arr, gate_spec = _mod_spec(*mods, layer, 2, tm, SEQ, decode)

    def col(c):
        return pl.BlockSpec((tm, D_MODEL), lambda i: (i, c))

    row = pl.BlockSpec((tm, D_MODEL), lambda i: (i, 0))
    wspec = pl.BlockSpec((None, D_MODEL, D_MODEL), lambda i: (layer, 0, 0))
    in_specs = [row, gate_spec, col(COL_CB), col(COL_CC), col(COL_CX), col(COL_GA), col(COL_GB), col(COL_GC),
                row, row, pl.BlockSpec((None, CONV_WIDTH, D_MODEL), lambda i: (layer, 0, 0)), wspec, wspec]
    args = [x, gate_arr, proj, proj, proj, proj, proj, proj, out_b, out_c, conv_w, w_conv_out, w_o]
    if decode:
        state2 = conv_state.reshape(DEPTH, m, (CONV_WIDTH - 1) * D_MODEL)
        in_specs += [pl.BlockSpec((None, m, D_MODEL), lambda i: (layer, 0, 0)),
                     pl.BlockSpec((None, m, D_MODEL), lambda i: (layer, 0, 1))]
        args += [state2, state2]
        return pl.pallas_call(
            _mix_decode_kernel,
            out_shape=(jax.ShapeDtypeStruct((m, D_MODEL), F32),
                       jax.ShapeDtypeStruct((m, (CONV_WIDTH - 1) * D_MODEL), F32)),
            grid=(1,),
            in_specs=in_specs,
            out_specs=(row, pl.BlockSpec((m, (CONV_WIDTH - 1) * D_MODEL), lambda i: (0, 0))),
            compiler_params=_params("arbitrary"),
        )(*args)
    tiles_per_seq = SEQ // tm
    return pl.pallas_call(
        functools.partial(_mix_prompt_kernel, tm=tm, tiles_per_seq=tiles_per_seq),
        out_shape=(jax.ShapeDtypeStruct((m, D_MODEL), F32),
                   jax.ShapeDtypeStruct((BATCH, SUBLANES, D_MODEL), F32)),
        grid=(m // tm,),
        in_specs=in_specs,
        out_specs=(row, pl.BlockSpec((None, SUBLANES, D_MODEL), lambda i: (i // tiles_per_seq, 0, 0))),
        scratch_shapes=[pltpu.VMEM((SUBLANES + tm, D_MODEL), F32)],
        compiler_params=_params("arbitrary"),
    )(*args)


FFN_CHUNKS = (MXU_DIM * (D_FF // MXU_DIM - D_FF // MXU_DIM // 2), MXU_DIM * (D_FF // MXU_DIM // 2))
assert sum(FFN_CHUNKS) == D_FF


def _ffn_kernel(x_ref, g_ref, sc_ref, sh_ref, gate_ref, w1_ref, w3_ref, w2_ref, gf_ref, o_ref, *, final_norm):
    x = x_ref[...]
    h = _mod_norm(x, g_ref[...], sc_ref[...], sh_ref[...]).astype(BF16)
    acc = None
    start = 0
    for width in FFN_CHUNKS:
        cols = slice(start, start + width)
        start += width
        a = (jax.nn.silu(jnp.dot(h, w1_ref[:, cols], preferred_element_type=F32))
             * jnp.dot(h, w3_ref[:, cols], preferred_element_type=F32))
        part = jnp.dot(a.astype(BF16), w2_ref[cols, :], preferred_element_type=F32)
        acc = part if acc is None else acc + part
    y = x + gate_ref[...] * acc
    if final_norm:
        y = y * lax.rsqrt(jnp.mean(y * y, axis=-1, keepdims=True) + EPS) * gf_ref[...]
    o_ref[...] = y


def _ffn(x, mods, layer, g_ffn, w1, w3, w2, g_final, decode):
    m = x.shape[0]
    tm = m if decode else 512
    sc_arr, sc_spec = _mod_spec(*mods, layer, 4, tm, SEQ, decode)
    sh_arr, sh_spec = _mod_spec(*mods, layer, 3, tm, SEQ, decode)
    gt_arr, gt_spec = _mod_spec(*mods, layer, 5, tm, SEQ, decode)
    row = pl.BlockSpec((tm, D_MODEL), lambda i: (i, 0))

    def resident(shape):
        return pl.BlockSpec((None,) + shape, lambda i: (layer, 0, 0), pipeline_mode=pl.Buffered(1))

    return pl.pallas_call(
        functools.partial(_ffn_kernel, final_norm=layer == DEPTH - 1),
        out_shape=jax.ShapeDtypeStruct((m, D_MODEL), F32),
        grid=(m // tm,),
        in_specs=[row, pl.BlockSpec((None, 1, D_MODEL), lambda i: (layer, 0, 0)), sc_spec, sh_spec, gt_spec,
                  resident((D_MODEL, D_FF)), resident((D_MODEL, D_FF)), resident((D_FF, D_MODEL)),
                  pl.BlockSpec((1, D_MODEL), lambda i: (0, 0))],
        out_specs=row,
        compiler_params=_params("parallel"),
    )(x, g_ffn, sc_arr, sh_arr, gt_arr, w1, w3, w2, g_final.reshape(1, D_MODEL))


def _trunk(x, mods, w, decode, state=None):
    ks, vs, convs, res, ims = [], [], [], [], []
    kv_stacks = None
    for l in range(DEPTH):
        mix, qkv = _in_proj(x, mods, l, w["g_mix"], w["w_in"], decode)
        if decode:
            out_b, s_re, s_im = _s5_decode(mix, state["ssm_re"], state["ssm_im"], l, w["s5"])
            out_c = _moba_decode(qkv, state["cache_k"], state["cache_v"], state["page_table"], l)
            ks.append(qkv[:, COL_K:COL_K + KV_DIM])
            vs.append(qkv[:, COL_V:COL_V + KV_DIM])
            x, conv_last = _mix(x, mods, l, mix, out_b, out_c, w["conv_w"], w["w_conv_out"], w["w_o"], True,
                                state["conv"])
            conv_last = conv_last.reshape(DEC_BATCH, CONV_WIDTH - 1, D_MODEL)
        else:
            out_b, s_re, s_im = _s5_prompt(mix.reshape(BATCH, SEQ, D_MIX), l, w["s5"])
            out_c, kv_stacks = _moba_prompt(qkv.reshape(BATCH, SEQ, D_QKV), l, kv_stacks)
            x, conv_last = _mix(x, mods, l, mix, out_b.reshape(BATCH * SEQ, D_MODEL),
                                out_c.reshape(BATCH * SEQ, D_MODEL), w["conv_w"], w["w_conv_out"], w["w_o"], False)
            conv_last = conv_last[:, SUBLANES - (CONV_WIDTH - 1):, :]
        x = _ffn(x, mods, l, w["g_ffn"], w["w_ff1"], w["w_ff3"], w["w_ff2"], w["g_final"], decode)
        convs.append(conv_last)
        res.append(s_re)
        ims.append(s_im)
    y = x
    k_all, v_all = (jnp.stack(ks), jnp.stack(vs)) if decode else kv_stacks
    return y, k_all, v_all, jnp.stack(convs), jnp.stack(res), jnp.stack(ims)


def kernel(x_prompt, x_sample, cache_k, cache_v, state_conv, state_ssm_re, state_ssm_im, page_table, c_prompt, c_sample, w_ada, b_ada, g_mix, w_in, conv_w, w_conv_out, ssm_lam_re, ssm_lam_im, ssm_log_step, ssm_b_re, ssm_b_im, ssm_c_re, ssm_c_im, ssm_d, w_glu_a, w_glu_b, w_o, g_ffn, w_ff1, w_ff3, w_ff2, g_final):
    n_pool = cache_k.shape[1]
    mod = _ada(jnp.concatenate([c_sample, c_prompt], axis=0), w_ada, b_ada)
    mods = (mod, mod.reshape(DEPTH, DEC_BATCH + BATCH, 6, 1, D_MODEL))

    lam_re, lam_im, bb_re, bb_im = _s5_discretize(
        ssm_lam_re, ssm_lam_im, ssm_log_step, jnp.swapaxes(ssm_b_re, 2, 3), jnp.swapaxes(ssm_b_im, 2, 3))
    s5w = (lam_re.reshape(DEPTH, 1, SSM_LANES), lam_im.reshape(DEPTH, 1, SSM_LANES),
           _block_diag_in(bb_re).astype(BF16), _block_diag_in(bb_im).astype(BF16),
           _block_diag_out(ssm_c_re).astype(BF16), _block_diag_out(ssm_c_im).astype(BF16),
           ssm_d.reshape(DEPTH, 1, D_MODEL), w_glu_a.astype(BF16), w_glu_b.astype(BF16))
    w = dict(g_mix=g_mix.reshape(DEPTH, 1, D_MODEL), w_in=w_in.astype(BF16), s5=s5w, conv_w=conv_w,
             w_conv_out=w_conv_out.astype(BF16), w_o=w_o.astype(BF16), g_ffn=g_ffn.reshape(DEPTH, 1, D_MODEL),
             w_ff1=w_ff1.astype(BF16), w_ff3=w_ff3.astype(BF16), w_ff2=w_ff2.astype(BF16), g_final=g_final)

    yp, kp, vp, cp, rp, ip = _trunk(x_prompt.reshape(BATCH * SEQ, D_MODEL), mods, w, False)
    state = dict(ssm_re=state_ssm_re.reshape(DEPTH, DEC_BATCH, SSM_LANES),
                 ssm_im=state_ssm_im.reshape(DEPTH, DEC_BATCH, SSM_LANES),
                 cache_k=cache_k.reshape(DEPTH, n_pool, PAGE_ROWS, HEAD_DIM),
                 cache_v=cache_v.reshape(DEPTH, n_pool, PAGE_ROWS, HEAD_DIM),
                 page_table=page_table, conv=state_conv)
    ys, ksm, vsm, cs, rs, is_ = _trunk(x_sample.reshape(DEC_BATCH, D_MODEL), mods, w, True, state)

    kv_p = (DEPTH, BATCH, SEQ, N_KV_HEADS, HEAD_DIM)
    kv_s = (DEPTH, DEC_BATCH, 1, N_KV_HEADS, HEAD_DIM)
    st_p = (DEPTH, BATCH, SSM_GROUPS, SSM_STATE)
    st_s = (DEPTH, DEC_BATCH, SSM_GROUPS, SSM_STATE)
    return (yp.reshape(BATCH, SEQ, D_MODEL), ys.reshape(DEC_BATCH, 1, D_MODEL),
            kp.reshape(kv_p), vp.reshape(kv_p), cp, rp.reshape(st_p), ip.reshape(st_p),
            ksm.reshape(kv_s), vsm.reshape(kv_s), cs, rs.reshape(st_s), is_.reshape(st_s))
```

```python
import functools

import jax
import jax.numpy as jnp
from jax import lax
from jax.experimental import pallas as pl
from jax.experimental.pallas import tpu as pltpu

F32 = jnp.float32
BF16 = jnp.bfloat16

D_MODEL = 1024
DEPTH = 4
SEQ = 2048
BATCH = 8
DEC_BATCH = 128
PAGE_SIZE = 128
EPS = 1e-6
CONV_WIDTH = 3
SSM_GROUP = 16
SSM_GROUPS = D_MODEL // SSM_GROUP
SSM_STATE = 64
SSM_LANES = SSM_GROUPS * SSM_STATE
HEAD_DIM = 128
N_HEADS = D_MODEL // HEAD_DIM
N_KV_HEADS = N_HEADS // 2
KV_GROUP = N_HEADS // N_KV_HEADS
KV_DIM = N_KV_HEADS * HEAD_DIM
KV_GROUP_SHIFT = KV_GROUP.bit_length() - 1
assert KV_GROUP == 1 << KV_GROUP_SHIFT and N_KV_HEADS & (N_KV_HEADS - 1) == 0
PAGE_ROWS = PAGE_SIZE * N_KV_HEADS
MOBA_BLOCK = 256
MOBA_TOPK = 3
ATTN_SCALE = HEAD_DIM ** -0.5
NEG = -1e30
D_FF = 2816
D_IN = 9216
COL_CB, COL_CC, COL_CX, COL_U, COL_GA, COL_GB, COL_GC = 0, 1, 2, 3, 4, 5, 6
D_MIX = 7 * D_MODEL
COL_Q = 0
COL_K = D_MODEL
COL_V = COL_K + KV_DIM
D_QKV = D_MODEL + 2 * KV_DIM
W_QKV = 4 * D_MODEL

LANES = 128
SUBLANES = 8
MXU_DIM = 256
VMEM_LIMIT = 56 * 1024 * 1024

S5_SLABS = D_MODEL // MXU_DIM
S5_SLAB_LANES = SSM_LANES // S5_SLABS


def _params(*sem):
    return pltpu.CompilerParams(dimension_semantics=sem, vmem_limit_bytes=VMEM_LIMIT)


def _ada_kernel(c_ref, w_ref, b_ref, o_ref):
    c = jax.nn.silu(c_ref[...]).astype(BF16)
    o_ref[...] = jnp.dot(c, w_ref[...].astype(BF16), preferred_element_type=F32) + b_ref[...]


def _ada(c_all, w_ada, b_ada):
    n_rows = c_all.shape[0]
    tn = 1536
    return pl.pallas_call(
        _ada_kernel,
        out_shape=jax.ShapeDtypeStruct((DEPTH, n_rows, 6 * D_MODEL), F32),
        grid=(DEPTH, 6 * D_MODEL // tn),
        in_specs=[
            pl.BlockSpec((n_rows, D_MODEL), lambda l, j: (0, 0)),
            pl.BlockSpec((None, D_MODEL, tn), lambda l, j: (l, 0, j)),
            pl.BlockSpec((None, 1, tn), lambda l, j: (l, 0, j)),
        ],
        out_specs=pl.BlockSpec((None, n_rows, tn), lambda l, j: (l, 0, j)),
        compiler_params=_params("parallel", "parallel"),
    )(c_all, w_ada, b_ada.reshape(DEPTH, 1, 6 * D_MODEL))


def _s5_disc_kernel(lre_ref, lim_ref, ls_ref, bre_ref, bim_ref, lbre_ref, lbim_ref, bbre_ref, bbim_ref):
    lre = lre_ref[...]
    lim = lim_ref[...]
    step = jnp.exp(ls_ref[...])
    mag = jnp.exp(lre * step)
    ang = lim * step
    lbre = mag * jnp.cos(ang)
    lbim = mag * jnp.sin(ang)
    lbre_ref[...] = lbre
    lbim_ref[...] = lbim
    nre = lbre - 1.0
    den = lre * lre + lim * lim
    cre = (nre * lre + lbim * lim) / den
    cim = (lbim * lre - nre * lim) / den
    bre = bre_ref[...]
    bim = bim_ref[...]
    bbre_ref[...] = cre * bre - cim * bim
    bbim_ref[...] = cre * bim + cim * bre


def _s5_discretize(lam_re, lam_im, log_step, b_re_t, b_im_t):
    g, p, i = SSM_GROUPS, SSM_STATE, SSM_GROUP
    gp = pl.BlockSpec((None, g, 1, p), lambda l: (l, 0, 0, 0))
    gip = pl.BlockSpec((None, g, i, p), lambda l: (l, 0, 0, 0))
    return pl.pallas_call(
        _s5_disc_kernel,
        out_shape=(jax.ShapeDtypeStruct((DEPTH, g, 1, p), F32), jax.ShapeDtypeStruct((DEPTH, g, 1, p), F32),
                   jax.ShapeDtypeStruct((DEPTH, g, i, p), F32), jax.ShapeDtypeStruct((DEPTH, g, i, p), F32)),
        grid=(DEPTH,),
        in_specs=[gp, gp, pl.BlockSpec((None, g, 1, 1), lambda l: (l, 0, 0, 0)), gip, gip],
        out_specs=(gp, gp, gip, gip),
        compiler_params=_params("parallel"),
    )(lam_re.reshape(DEPTH, g, 1, p), lam_im.reshape(DEPTH, g, 1, p), log_step.reshape(DEPTH, g, 1, 1),
      b_re_t, b_im_t)


def _block_diag_in(b_t):
    return _tile_diag(b_t.reshape(DEPTH, S5_SLABS, MXU_DIM, SSM_STATE), SSM_GROUP, SSM_STATE)


def _block_diag_out(c):
    ct = jnp.swapaxes(c, 2, 3).reshape(DEPTH, S5_SLABS, S5_SLAB_LANES, SSM_GROUP)
    return _tile_diag(ct, SSM_STATE, SSM_GROUP)


def _tile_diag(a, rows_per_group, cols_per_group):
    gs = SSM_GROUPS // S5_SLABS
    tiled = jnp.tile(a, (1, 1, 1, gs))
    row_g = jnp.arange(gs * rows_per_group)[:, None] // rows_per_group
    col_g = jnp.arange(gs * cols_per_group)[None, :] // cols_per_group
    return jnp.where(row_g == col_g, tiled, 0.0)


def _mod_norm(x, g, scale, shift):
    y = x * lax.rsqrt(jnp.mean(x * x, axis=-1, keepdims=True) + EPS)
    return (y * g) * (1.0 + scale) + shift


IN_PROJ_CHUNK = 1024


def _in_proj_kernel(x_ref, g_ref, sc_ref, sh_ref, w_ref, mix_ref, qkv_ref):
    h = _mod_norm(x_ref[...], g_ref[...], sc_ref[...], sh_ref[...]).astype(BF16)
    nc = IN_PROJ_CHUNK
    for n in range(D_MIX // nc):
        src = n * nc if n * nc < W_QKV else n * nc + D_QKV
        mix_ref[:, n * nc:(n + 1) * nc] = jnp.dot(
            h, w_ref[:, src:src + nc], preferred_element_type=F32).astype(BF16)
    for n in range(D_QKV // nc):
        src = W_QKV + n * nc
        qkv_ref[:, n * nc:(n + 1) * nc] = jnp.dot(h, w_ref[:, src:src + nc], preferred_element_type=F32)


def _mod_spec(mod, mod5, layer, chunk, tm, rows_per_seq, decode):
    if decode:
        return mod, pl.BlockSpec((None, tm, D_MODEL), lambda i, *_: (layer, 0, chunk))
    return mod5, pl.BlockSpec((None, None, None, 1, D_MODEL),
                              lambda i, *_: (layer, (i * tm) // rows_per_seq, chunk, 0, 0))


def _in_proj(x, mods, layer, g_mix, w_in, decode):
    m = x.shape[0]
    tm = m if decode else 512
    sc_arr, sc_spec = _mod_spec(*mods, layer, 1, tm, SEQ, decode)
    sh_arr, sh_spec = _mod_spec(*mods, layer, 0, tm, SEQ, decode)
    return pl.pallas_call(
        _in_proj_kernel,
        out_shape=(jax.ShapeDtypeStruct((m, D_MIX), BF16), jax.ShapeDtypeStruct((m, D_QKV), F32)),
        grid=(m // tm,),
        in_specs=[
            pl.BlockSpec((tm, D_MODEL), lambda i: (i, 0)),
            pl.BlockSpec((None, 1, D_MODEL), lambda i: (layer, 0, 0)),
            sc_spec, sh_spec,
            pl.BlockSpec((None, D_MODEL, D_IN), lambda i: (layer, 0, 0), pipeline_mode=pl.Buffered(1)),
        ],
        out_specs=(pl.BlockSpec((tm, D_MIX), lambda i: (i, 0)), pl.BlockSpec((tm, D_QKV), lambda i: (i, 0))),
        compiler_params=_params("parallel"),
    )(x, g_mix, sc_arr, sh_arr, w_in)


def _s5_readout(u_slab, sre_slab, sim_slab, cre_ref, cim_ref, d_ref, s):
    y = (jnp.dot(sre_slab.astype(BF16), cre_ref[s], preferred_element_type=F32)
         - jnp.dot(sim_slab.astype(BF16), cim_ref[s], preferred_element_type=F32))
    y = y + u_slab * d_ref[:, s * MXU_DIM:(s + 1) * MXU_DIM]
    return jax.nn.gelu(y, approximate=True)


def _glu(g, wa_ref, wb_ref):
    gb = g.astype(BF16)
    return (jnp.dot(gb, wa_ref[...], preferred_element_type=F32)
            * jax.nn.sigmoid(jnp.dot(gb, wb_ref[...], preferred_element_type=F32)))


SCAN_LANES = 512


def _s5_prompt_kernel(u_ref, lre_ref, lim_ref, bre_ref, bim_ref, cre_ref, cim_ref, d_ref, wa_ref, wb_ref,
                      ob_ref, fre_ref, fim_ref,
                      us_ref, sre_ref, sim_ref, os_ref, st_ref, *, tt):
    n_lane_slabs = D_MODEL // LANES

    @pl.when(pl.program_id(0) == 0)
    def _():
        st_ref[...] = jnp.zeros_like(st_ref)

    for b in range(BATCH):
        for sl in range(n_lane_slabs):
            us_ref[sl, pl.ds(b, tt, stride=BATCH), :] = u_ref[b, :, sl * LANES:(sl + 1) * LANES].astype(F32)

    def u_slab(s):
        per = MXU_DIM // LANES
        return jnp.concatenate([us_ref[per * s + k] for k in range(per)], axis=1)

    gs = []
    for s in range(S5_SLABS):
        ub = u_slab(s).astype(BF16)
        slab = slice(s * S5_SLAB_LANES, (s + 1) * S5_SLAB_LANES)
        sre_ref[:, slab] = jnp.dot(ub, bre_ref[s], preferred_element_type=F32)
        sim_ref[:, slab] = jnp.dot(ub, bim_ref[s], preferred_element_type=F32)
        for c in range(S5_SLAB_LANES // SCAN_LANES):
            cols = slice(s * S5_SLAB_LANES + c * SCAN_LANES, s * S5_SLAB_LANES + (c + 1) * SCAN_LANES)
            lre = jnp.broadcast_to(lre_ref[:, cols], (BATCH, SCAN_LANES))
            lim = jnp.broadcast_to(lim_ref[:, cols], (BATCH, SCAN_LANES))
            pre, pim = st_ref[0, :, cols], st_ref[1, :, cols]
            for t in range(tt):
                rows = slice(t * BATCH, (t + 1) * BATCH)
                pre, pim = (lre * pre - lim * pim + sre_ref[rows, cols],
                            lre * pim + lim * pre + sim_ref[rows, cols])
                sre_ref[rows, cols] = pre
                sim_ref[rows, cols] = pim
            st_ref[0, :, cols] = pre
            st_ref[1, :, cols] = pim
        gs.append(_s5_readout(u_slab(s), sre_ref[:, slab], sim_ref[:, slab], cre_ref, cim_ref, d_ref, s))

    fre_ref[...] = st_ref[0]
    fim_ref[...] = st_ref[1]

    ob = _glu(jnp.concatenate(gs, axis=1), wa_ref, wb_ref)

    for sl in range(n_lane_slabs):
        os_ref[sl] = ob[:, sl * LANES:(sl + 1) * LANES]
    for b in range(BATCH):
        for sl in range(n_lane_slabs):
            ob_ref[b, :, sl * LANES:(sl + 1) * LANES] = os_ref[sl, pl.ds(b, tt, stride=BATCH), :].astype(BF16)


def _s5_decode_kernel(u_ref, s0re_ref, s0im_ref, lre_ref, lim_ref, bre_ref, bim_ref, cre_ref, cim_ref, d_ref,
                      wa_ref, wb_ref, ob_ref, fre_ref, fim_ref):
    gs = []
    for s in range(S5_SLABS):
        cols = slice(s * S5_SLAB_LANES, (s + 1) * S5_SLAB_LANES)
        ub = u_ref[:, s * MXU_DIM:(s + 1) * MXU_DIM]
        u = ub.astype(F32)
        lre, lim = lre_ref[:, cols], lim_ref[:, cols]
        pre, pim = s0re_ref[:, cols], s0im_ref[:, cols]
        nre = lre * pre - lim * pim + jnp.dot(ub, bre_ref[s], preferred_element_type=F32)
        nim = lre * pim + lim * pre + jnp.dot(ub, bim_ref[s], preferred_element_type=F32)
        fre_ref[:, cols] = nre
        fim_ref[:, cols] = nim
        gs.append(_s5_readout(u, nre, nim, cre_ref, cim_ref, d_ref, s))
    ob_ref[...] = _glu(jnp.concatenate(gs, axis=1), wa_ref, wb_ref)


def _s5_weight_specs(layer, n_grid):
    def const(shape, *idx):
        return pl.BlockSpec(shape, lambda *_: idx, pipeline_mode=pl.Buffered(1))

    return [
        const((None, 1, SSM_LANES), layer, 0, 0),
        const((None, 1, SSM_LANES), layer, 0, 0),
        const((None, S5_SLABS, MXU_DIM, S5_SLAB_LANES), layer, 0, 0, 0),
        const((None, S5_SLABS, MXU_DIM, S5_SLAB_LANES), layer, 0, 0, 0),
        const((None, S5_SLABS, S5_SLAB_LANES, MXU_DIM), layer, 0, 0, 0),
        const((None, S5_SLABS, S5_SLAB_LANES, MXU_DIM), layer, 0, 0, 0),
        const((None, 1, D_MODEL), layer, 0, 0),
        const((None, D_MODEL, D_MODEL), layer, 0, 0),
        const((None, D_MODEL, D_MODEL), layer, 0, 0),
    ]


def _s5_prompt(mix3, layer, s5w):
    tt = 64
    rows = tt * BATCH
    kern = functools.partial(_s5_prompt_kernel, tt=tt)
    state = jax.ShapeDtypeStruct((BATCH, SSM_LANES), F32)
    state_spec = pl.BlockSpec((BATCH, SSM_LANES), lambda i: (0, 0))
    return pl.pallas_call(
        kern,
        out_shape=(jax.ShapeDtypeStruct((BATCH, SEQ, D_MODEL), BF16), state, state),
        grid=(SEQ // tt,),
        in_specs=[pl.BlockSpec((BATCH, tt, D_MODEL), lambda i: (0, i, COL_U))] + _s5_weight_specs(layer, 1),
        out_specs=(pl.BlockSpec((BATCH, tt, D_MODEL), lambda i: (0, i, 0)), state_spec, state_spec),
        scratch_shapes=[
            pltpu.VMEM((D_MODEL // LANES, rows, LANES), F32),
            pltpu.VMEM((rows, SSM_LANES), F32),
            pltpu.VMEM((rows, SSM_LANES), F32),
            pltpu.VMEM((D_MODEL // LANES, rows, LANES), F32),
            pltpu.VMEM((2, BATCH, SSM_LANES), F32),
        ],
        compiler_params=_params("arbitrary"),
    )(mix3, *s5w)


def _s5_decode(mix, s0re, s0im, layer, s5w):
    m = mix.shape[0]
    state = jax.ShapeDtypeStruct((m, SSM_LANES), F32)
    full = pl.BlockSpec((m, SSM_LANES), lambda i: (0, 0))
    return pl.pallas_call(
        _s5_decode_kernel,
        out_shape=(jax.ShapeDtypeStruct((m, D_MODEL), F32), state, state),
        grid=(1,),
        in_specs=[pl.BlockSpec((m, D_MODEL), lambda i: (0, COL_U)),
                  pl.BlockSpec((None, m, SSM_LANES), lambda i: (layer, 0, 0)),
                  pl.BlockSpec((None, m, SSM_LANES), lambda i: (layer, 0, 0))] + _s5_weight_specs(layer, 1),
        out_specs=(pl.BlockSpec((m, D_MODEL), lambda i: (0, 0)), full, full),
        compiler_params=_params("arbitrary"),
    )(mix, s0re, s0im, *s5w)


def _topk_select(gate, valid, n_candidates):
    lane = lax.broadcasted_iota(jnp.int32, gate.shape, 1)
    g = jnp.where(valid, gate, NEG)
    rank = jnp.zeros(gate.shape, jnp.int32)
    for jp in range(n_candidates):
        col = g[:, jp:jp + 1]
        rank = rank + jnp.where(col > g, 1, jnp.where(col == g, jnp.where(lane > jp, 1, 0), 0))
    return jnp.where(valid, jnp.where(rank < MOBA_TOPK, 1.0, 0.0), 0.0)


def _moba_prompt_kernel(q_ref, k_ref, v_ref, *refs):
    o_ref, ko_ref, vo_ref = refs[-3:]
    nb = SEQ // MOBA_BLOCK
    k = k_ref[...]
    v = v_ref[...]
    head_rows = pl.ds(pl.program_id(1), SEQ, stride=N_KV_HEADS)
    ko_ref[head_rows, :] = k
    vo_ref[head_rows, :] = v
    assert nb <= SUBLANES
    vb = jnp.concatenate([v.astype(BF16), jnp.ones((SEQ, HEAD_DIM), BF16)], axis=1)
    key_blk = lax.broadcasted_iota(jnp.int32, (SEQ, LANES), 0) // MOBA_BLOCK
    onehot = jnp.where(key_blk == lax.broadcasted_iota(jnp.int32, (SEQ, LANES), 1), 1.0, 0.0)
    kb = jnp.concatenate([k.astype(BF16), onehot.astype(BF16)], axis=1)
    kmean = jnp.mean(k.reshape(nb, MOBA_BLOCK, HEAD_DIM), axis=1).astype(BF16)
    rows = KV_GROUP * MOBA_BLOCK
    blk = lax.broadcasted_iota(jnp.int32, (nb, rows), 0)
    q_in_blk = lax.broadcasted_iota(jnp.int32, (rows, MOBA_BLOCK), 0) & (MOBA_BLOCK - 1)
    k_in_blk = lax.broadcasted_iota(jnp.int32, (rows, MOBA_BLOCK), 1)
    causal = k_in_blk <= q_in_blk
    nt = (((1,), (1,)), ((), ()))
    for i in range(nb):
        q2 = q_ref[i * MOBA_BLOCK:(i + 1) * MOBA_BLOCK, :]
        q = jnp.concatenate([q2[:, h * HEAD_DIM:(h + 1) * HEAD_DIM] for h in range(KV_GROUP)], axis=0)
        nk = (i + 1) * MOBA_BLOCK
        if i > 0:
            gate = lax.dot_general(kmean, q.astype(BF16), nt, preferred_element_type=F32)
            g = jnp.where(blk < i, gate, NEG)
            rank = jnp.zeros((nb, rows), jnp.int32)
            for jp in range(i):
                other = g[jp:jp + 1, :]
                rank = rank + jnp.where(other > g, 1, jnp.where(other == g, jnp.where(blk > jp, 1, 0), 0))
            bias = jnp.where(blk < i, jnp.where(rank < MOBA_TOPK, 0.0, NEG), 0.0)
            bias = jnp.concatenate([bias, jnp.zeros((LANES - nb, rows), F32)], axis=0).T
        else:
            bias = jnp.zeros((rows, LANES), F32)
        qa = jnp.concatenate([(q * ATTN_SCALE).astype(BF16), bias.astype(BF16)], axis=1)
        s = lax.dot_general(qa, kb[:nk], nt, preferred_element_type=F32)
        diag = jnp.where(causal, s[:, i * MOBA_BLOCK:], NEG)
        s = jnp.concatenate([s[:, :i * MOBA_BLOCK], diag], axis=1) if i > 0 else diag
        m = jnp.max(s, axis=-1, keepdims=True)
        p = jnp.exp(s - m)
        o = jnp.dot(p.astype(BF16), vb[:nk], preferred_element_type=F32)
        o = o[:, :HEAD_DIM] / o[:, HEAD_DIM:]
        for h in range(KV_GROUP):
            o_ref[i * MOBA_BLOCK:(i + 1) * MOBA_BLOCK, h * HEAD_DIM:(h + 1) * HEAD_DIM] = (
                o[h * MOBA_BLOCK:(h + 1) * MOBA_BLOCK].astype(BF16))


def _moba_prompt(proj3, layer, kv_stacks):
    gw = KV_GROUP * HEAD_DIM
    kv = jax.ShapeDtypeStruct((DEPTH, BATCH, SEQ * N_KV_HEADS, HEAD_DIM), F32)
    kv_spec = pl.BlockSpec((None, None, SEQ * N_KV_HEADS, HEAD_DIM), lambda b, h: (layer, b, 0, 0))
    in_specs = [
        pl.BlockSpec((None, SEQ, gw), lambda b, h: (b, 0, COL_Q // gw + h)),
        pl.BlockSpec((None, SEQ, HEAD_DIM), lambda b, h: (b, 0, COL_K // HEAD_DIM + h)),
        pl.BlockSpec((None, SEQ, HEAD_DIM), lambda b, h: (b, 0, COL_V // HEAD_DIM + h)),
    ]
    args = [proj3, proj3, proj3]
    aliases = {}
    if kv_stacks is not None:
        in_specs += [pl.BlockSpec(memory_space=pl.ANY)] * 2
        args += list(kv_stacks)
        aliases = {3: 1, 4: 2}
    out_c, k_stack, v_stack = pl.pallas_call(
        _moba_prompt_kernel,
        out_shape=(jax.ShapeDtypeStruct((BATCH, SEQ, D_MODEL), BF16), kv, kv),
        grid=(BATCH, N_KV_HEADS),
        in_specs=in_specs,
        out_specs=(pl.BlockSpec((None, SEQ, gw), lambda b, h: (b, 0, h)), kv_spec, kv_spec),
        input_output_aliases=aliases,
        compiler_params=_params("parallel", "arbitrary"),
    )(*args)
    return out_c, (k_stack, v_stack)


def _moba_decode_kernel(pt_ref, q_ref, kn_ref, vn_ref, *refs, n_pages):
    del pt_ref
    assert SUBLANES == 2 * N_KV_HEADS
    k_pages = refs[:n_pages]
    v_pages = refs[n_pages:2 * n_pages]
    o_ref = refs[2 * n_pages]
    per = MOBA_BLOCK // PAGE_SIZE
    nb = n_pages // per
    nt = (((1,), (1,)), ((), ()))
    q = q_ref[...]
    qs = (q * ATTN_SCALE).astype(BF16)
    head_kv = lax.broadcasted_iota(jnp.int32, (N_HEADS, PAGE_ROWS), 0) >> KV_GROUP_SHIFT
    row_kv = lax.broadcasted_iota(jnp.int32, (N_HEADS, PAGE_ROWS), 1) & (N_KV_HEADS - 1)
    own_kv = row_kv == head_kv
    scores, ksums = [], []
    for pg in range(n_pages):
        kp = k_pages[pg][...]
        scores.append(lax.dot_general(qs, kp.astype(BF16), nt, preferred_element_type=F32))
        ksums.append(jnp.sum(kp.reshape(PAGE_ROWS // SUBLANES, SUBLANES, HEAD_DIM), axis=0))
    kmeans = []
    for j in range(nb):
        ks = ksums[per * j]
        for r in range(1, per):
            ks = ks + ksums[per * j + r]
        ks = ks + pltpu.roll(ks, N_KV_HEADS, 0)
        kmeans.append(ks * (1.0 / MOBA_BLOCK))
    kmean = jnp.concatenate(kmeans, axis=0).astype(BF16)
    gate = lax.dot_general(q.astype(BF16), kmean, nt, preferred_element_type=F32)
    n_cand = nb * SUBLANES
    cand = lax.broadcasted_iota(jnp.int32, (N_HEADS, n_cand), 1) & (SUBLANES - 1)
    cand_head_kv = lax.broadcasted_iota(jnp.int32, (N_HEADS, n_cand), 0) >> KV_GROUP_SHIFT
    sel = _topk_select(gate, cand == cand_head_kv, n_cand)
    masked = []
    for j in range(nb):
        on = jnp.max(sel[:, j * SUBLANES:(j + 1) * SUBLANES], axis=-1, keepdims=True) > 0.0
        for r in range(per):
            masked.append(jnp.where(own_kv, jnp.where(on, scores[per * j + r], NEG), NEG))
    kn = kn_ref[...]
    vn = vn_ref[...]
    kn_h = jnp.concatenate([kn[h // KV_GROUP:h // KV_GROUP + 1] for h in range(N_HEADS)], axis=0)
    vn_h = jnp.concatenate([vn[h // KV_GROUP:h // KV_GROUP + 1] for h in range(N_HEADS)], axis=0)
    s_own = jnp.sum(q * ATTN_SCALE * kn_h, axis=-1, keepdims=True)
    m = jnp.maximum(jnp.max(functools.reduce(jnp.maximum, masked), axis=-1, keepdims=True), s_own)
    p_own = jnp.exp(s_own - m)
    acc = p_own * vn_h
    p_sum = jnp.zeros((N_HEADS, PAGE_ROWS), F32)
    for pg in range(n_pages):
        p = jnp.exp(masked[pg] - m)
        p_sum = p_sum + p
        acc = acc + jnp.dot(p.astype(BF16), v_pages[pg][...].astype(BF16), preferred_element_type=F32)
    l = jnp.sum(p_sum, axis=-1, keepdims=True) + p_own
    o_ref[...] = acc / l


def _moba_decode(proj, cache_k, cache_v, page_table, layer):
    m = proj.shape[0]
    n_pages = page_table.shape[1]
    q = proj[:, COL_Q:COL_Q + D_MODEL].reshape(m, N_HEADS, HEAD_DIM)
    kn = proj[:, COL_K:COL_K + KV_DIM].reshape(m, N_KV_HEADS, HEAD_DIM)
    vn = proj[:, COL_V:COL_V + KV_DIM].reshape(m, N_KV_HEADS, HEAD_DIM)

    def page_spec(pg):
        return pl.BlockSpec((None, None, PAGE_ROWS, HEAD_DIM), lambda b, pt: (layer, pt[b, pg], 0, 0))

    grid_spec = pltpu.PrefetchScalarGridSpec(
        num_scalar_prefetch=1,
        grid=(m,),
        in_specs=[
            pl.BlockSpec((None, N_HEADS, HEAD_DIM), lambda b, pt: (b, 0, 0)),
            pl.BlockSpec((None, N_KV_HEADS, HEAD_DIM), lambda b, pt: (b, 0, 0)),
            pl.BlockSpec((None, N_KV_HEADS, HEAD_DIM), lambda b, pt: (b, 0, 0)),
        ] + [page_spec(pg) for pg in range(n_pages)] + [page_spec(pg) for pg in range(n_pages)],
        out_specs=pl.BlockSpec((None, N_HEADS, HEAD_DIM), lambda b, pt: (b, 0, 0)),
    )
    out = pl.pallas_call(
        functools.partial(_moba_decode_kernel, n_pages=n_pages),
        out_shape=jax.ShapeDtypeStruct((m, N_HEADS, HEAD_DIM), F32),
        grid_spec=grid_spec,
        compiler_params=_params("parallel"),
    )(page_table, q, kn, vn, *([cache_k] * n_pages), *([cache_v] * n_pages))
    return out.reshape(m, D_MODEL)


def _sigmoid(x):
    return 0.5 * jnp.tanh(0.5 * x) + 0.5


def _merge_tail(x, gate, cb, zc, ga, gb, gc, ob, oc, wc_ref, wo_ref):
    out_a = jnp.dot((cb.astype(F32) * zc).astype(BF16), wc_ref[...], preferred_element_type=F32)
    merged = (_sigmoid(ga.astype(F32)) * out_a + _sigmoid(gb.astype(F32)) * ob.astype(F32)
              + _sigmoid(gc.astype(F32)) * oc.astype(F32))
    return x + gate * jnp.dot(merged.astype(BF16), wo_ref[...], preferred_element_type=F32)


def _mix_prompt_kernel(x_ref, gate_ref, cb_ref, cc_ref, cx_ref, ga_ref, gb_ref, gc_ref, ob_ref, oc_ref,
                       cw_ref, wc_ref, wo_ref, xo_ref, last_ref, z_ref, *, tm, tiles_per_seq):
    tail = SUBLANES

    @pl.when(pl.program_id(0) % tiles_per_seq == 0)
    def _():
        z_ref[0:tail, :] = jnp.zeros((tail, D_MODEL), F32)

    z = cc_ref[...].astype(F32) * cx_ref[...].astype(F32)
    z_ref[tail:tail + tm, :] = z
    w = cw_ref[...]
    zc = (z_ref[tail - 2:tail - 2 + tm, :] * w[0:1] + z_ref[tail - 1:tail - 1 + tm, :] * w[1:2] + z * w[2:3])
    z_last = z[tm - tail:, :]
    z_ref[0:tail, :] = z_last
    last_ref[...] = z_last
    xo_ref[...] = _merge_tail(x_ref[...], gate_ref[...], cb_ref[...], zc, ga_ref[...], gb_ref[...], gc_ref[...],
                              ob_ref[...], oc_ref[...], wc_ref, wo_ref)


def _mix_decode_kernel(x_ref, gate_ref, cb_ref, cc_ref, cx_ref, ga_ref, gb_ref, gc_ref, ob_ref, oc_ref,
                       cw_ref, wc_ref, wo_ref, p0_ref, p1_ref, xo_ref, last_ref):
    z = cc_ref[...].astype(F32) * cx_ref[...].astype(F32)
    w = cw_ref[...]
    p1 = p1_ref[...]
    zc = p0_ref[...] * w[0:1] + p1 * w[1:2] + z * w[2:3]
    last_ref[:, 0:D_MODEL] = p1
    last_ref[:, D_MODEL:2 * D_MODEL] = z
    xo_ref[...] = _merge_tail(x_ref[...], gate_ref[...], cb_ref[...], zc, ga_ref[...], gb_ref[...], gc_ref[...],
                              ob_ref[...], oc_ref[...], wc_ref, wo_ref)


def _mix(x, mods, layer, proj, out_b, out_c, conv_w, w_conv_out, w_o, decode, conv_state=None):
    m = x.shape[0]
    tm = m if decode else 512
    gate_arr, gate_spec = _mod_spec(*mods, layer, 2, tm, SEQ, decode)

    def col(c):
        return pl.BlockSpec((tm, D_MODEL), lambda i: (i, c))

    row = pl.BlockSpec((tm, D_MODEL), lambda i: (i, 0))
    wspec = pl.BlockSpec((None, D_MODEL, D_MODEL), lambda i: (layer, 0, 0))
    in_specs = [row, gate_spec, col(COL_CB), col(COL_CC), col(COL_CX), col(COL_GA), col(COL_GB), col(COL_GC),
                row, row, pl.BlockSpec((None, CONV_WIDTH, D_MODEL), lambda i: (layer, 0, 0)), wspec, wspec]
    args = [x, gate_arr, proj, proj, proj, proj, proj, proj, out_b, out_c, conv_w, w_conv_out, w_o]
    if decode:
        state2 = conv_state.reshape(DEPTH, m, (CONV_WIDTH - 1) * D_MODEL)
        in_specs += [pl.BlockSpec((None, m, D_MODEL), lambda i: (layer, 0, 0)),
                     pl.BlockSpec((None, m, D_MODEL), lambda i: (layer, 0, 1))]
        args += [state2, state2]
        return pl.pallas_call(
            _mix_decode_kernel,
            out_shape=(jax.ShapeDtypeStruct((m, D_MODEL), F32),
                       jax.ShapeDtypeStruct((m, (CONV_WIDTH - 1) * D_MODEL), F32)),
            grid=(1,),
            in_specs=in_specs,
            out_specs=(row, pl.BlockSpec((m, (CONV_WIDTH - 1) * D_MODEL), lambda i: (0, 0))),
            compiler_params=_params("arbitrary"),
        )(*args)
    tiles_per_seq = SEQ // tm
    return pl.pallas_call(
        functools.partial(_mix_prompt_kernel, tm=tm, tiles_per_seq=tiles_per_seq),
        out_shape=(jax.ShapeDtypeStruct((m, D_MODEL), F32),
                   jax.ShapeDtypeStruct((BATCH, SUBLANES, D_MODEL), F32)),
        grid=(m // tm,),
        in_specs=in_specs,
        out_specs=(row, pl.BlockSpec((None, SUBLANES, D_MODEL), lambda i: (i // tiles_per_seq, 0, 0))),
        scratch_shapes=[pltpu.VMEM((SUBLANES + tm, D_MODEL), F32)],
        compiler_params=_params("arbitrary"),
    )(*args)


FFN_CHUNKS = (MXU_DIM * (D_FF // MXU_DIM - D_FF // MXU_DIM // 2), MXU_DIM * (D_FF // MXU_DIM // 2))
assert sum(FFN_CHUNKS) == D_FF


def _ffn_kernel(x_ref, g_ref, sc_ref, sh_ref, gate_ref, w1_ref, w3_ref, w2_ref, gf_ref, o_ref, *, final_norm):
    x = x_ref[...]
    h = _mod_norm(x, g_ref[...], sc_ref[...], sh_ref[...]).astype(BF16)
    acc = None
    start = 0
    for width in FFN_CHUNKS:
        cols = slice(start, start + width)
        start += width
        a = (jax.nn.silu(jnp.dot(h, w1_ref[:, cols], preferred_element_type=F32))
             * jnp.dot(h, w3_ref[:, cols], preferred_element_type=F32))
        part = jnp.dot(a.astype(BF16), w2_ref[cols, :], preferred_element_type=F32)
        acc = part if acc is None else acc + part
    y = x + gate_ref[...] * acc
    if final_norm:
        y = y * lax.rsqrt(jnp.mean(y * y, axis=-1, keepdims=True) + EPS) * gf_ref[...]
    o_ref[...] = y


def _ffn(x, mods, layer, g_ffn, w1, w3, w2, g_final, decode):
    m = x.shape[0]
    tm = m if decode else 512
    sc_arr, sc_spec = _mod_spec(*mods, layer, 4, tm, SEQ, decode)
    sh_arr, sh_spec = _mod_spec(*mods, layer, 3, tm, SEQ, decode)
    gt_arr, gt_spec = _mod_spec(*mods, layer, 5, tm, SEQ, decode)
    row = pl.BlockSpec((tm, D_MODEL), lambda i: (i, 0))

    def resident(shape):
        return pl.BlockSpec((None,) + shape, lambda i: (layer, 0, 0), pipeline_mode=pl.Buffered(1))

    return pl.pallas_call(
        functools.partial(_ffn_kernel, final_norm=layer == DEPTH - 1),
        out_shape=jax.ShapeDtypeStruct((m, D_MODEL), F32),
        grid=(m // tm,),
        in_specs=[row, pl.BlockSpec((None, 1, D_MODEL), lambda i: (layer, 0, 0)), sc_spec, sh_spec, gt_spec,
                  resident((D_MODEL, D_FF)), resident((D_MODEL, D_FF)), resident((D_FF, D_MODEL)),
                  pl.BlockSpec((1, D_MODEL), lambda i: (0, 0))],
        out_specs=row,
        compiler_params=_params("parallel"),
    )(x, g_ffn, sc_arr, sh_arr, gt_arr, w1, w3, w2, g_final.reshape(1, D_MODEL))


def _trunk(x, mods, w, decode, state=None):
    ks, vs, convs, res, ims = [], [], [], [], []
    kv_stacks = None
    for l in range(DEPTH):
        mix, qkv = _in_proj(x, mods, l, w["g_mix"], w["w_in"], decode)
        if decode:
            out_b, s_re, s_im = _s5_decode(mix, state["ssm_re"], state["ssm_im"], l, w["s5"])
            out_c = _moba_decode(qkv, state["cache_k"], state["cache_v"], state["page_table"], l)
            ks.append(qkv[:, COL_K:COL_K + KV_DIM])
            vs.append(qkv[:, COL_V:COL_V + KV_DIM])
            x, conv_last = _mix(x, mods, l, mix, out_b, out_c, w["conv_w"], w["w_conv_out"], w["w_o"], True,
                                state["conv"])
            conv_last = conv_last.reshape(DEC_BATCH, CONV_WIDTH - 1, D_MODEL)
        else:
            out_b, s_re, s_im = _s5_prompt(mix.reshape(BATCH, SEQ, D_MIX), l, w["s5"])
            out_c, kv_stacks = _moba_prompt(qkv.reshape(BATCH, SEQ, D_QKV), l, kv_stacks)
            x, conv_last = _mix(x, mods, l, mix, out_b.reshape(BATCH * SEQ, D_MODEL),
                                out_c.reshape(BATCH * SEQ, D_MODEL), w["conv_w"], w["w_conv_out"], w["w_o"], False)
            conv_last = conv_last[:, SUBLANES - (CONV_WIDTH - 1):, :]
        x = _ffn(x, mods, l, w["g_ffn"], w["w_ff1"], w["w_ff3"], w["w_ff2"], w["g_final"], decode)
        convs.append(conv_last)
        res.append(s_re)
        ims.append(s_im)
    y = x
    k_all, v_all = (jnp.stack(ks), jnp.stack(vs)) if decode else kv_stacks
    return y, k_all, v_all, jnp.stack(convs), jnp.stack(res), jnp.stack(ims)


def kernel(x_prompt, x_sample, cache_k, cache_v, state_conv, state_ssm_re, state_ssm_im, page_table, c_prompt, c_sample, w_ada, b_ada, g_mix, w_in, conv_w, w_conv_out, ssm_lam_re, ssm_lam_im, ssm_log_step, ssm_b_re, ssm_b_im, ssm_c_re, ssm_c_im, ssm_d, w_glu_a, w_glu_b, w_o, g_ffn, w_ff1, w_ff3, w_ff2, g_final):
    n_pool = cache_k.shape[1]
    mod = _ada(jnp.concatenate([c_sample, c_prompt], axis=0), w_ada, b_ada)
    mods = (mod, mod[:, DEC_BATCH:].reshape(DEPTH, BATCH, 6, 1, D_MODEL))

    lam_re, lam_im, bb_re, bb_im = _s5_discretize(
        ssm_lam_re, ssm_lam_im, ssm_log_step, jnp.swapaxes(ssm_b_re, 2, 3), jnp.swapaxes(ssm_b_im, 2, 3))
    s5w = (lam_re.reshape(DEPTH, 1, SSM_LANES), lam_im.reshape(DEPTH, 1, SSM_LANES),
           _block_diag_in(bb_re).astype(BF16), _block_diag_in(bb_im).astype(BF16),
           _block_diag_out(ssm_c_re).astype(BF16), _block_diag_out(ssm_c_im).astype(BF16),
           ssm_d.reshape(DEPTH, 1, D_MODEL), w_glu_a.astype(BF16), w_glu_b.astype(BF16))
    w = dict(g_mix=g_mix.reshape(DEPTH, 1, D_MODEL), w_in=w_in.astype(BF16), s5=s5w, conv_w=conv_w,
             w_conv_out=w_conv_out.astype(BF16), w_o=w_o.astype(BF16), g_ffn=g_ffn.reshape(DEPTH, 1, D_MODEL),
             w_ff1=w_ff1.astype(BF16), w_ff3=w_ff3.astype(BF16), w_ff2=w_ff2.astype(BF16), g_final=g_final)

    yp, kp, vp, cp, rp, ip = _trunk(x_prompt.reshape(BATCH * SEQ, D_MODEL), mods, w, False)
    state = dict(ssm_re=state_ssm_re.reshape(DEPTH, DEC_BATCH, SSM_LANES),
                 ssm_im=state_ssm_im.reshape(DEPTH, DEC_BATCH, SSM_LANES),
                 cache_k=cache_k.reshape(DEPTH, n_pool, PAGE_ROWS, HEAD_DIM),
                 cache_v=cache_v.reshape(DEPTH, n_pool, PAGE_ROWS, HEAD_DIM),
                 page_table=page_table, conv=state_conv)
    ys, ksm, vsm, cs, rs, is_ = _trunk(x_sample.reshape(DEC_BATCH, D_MODEL), mods, w, True, state)

    kv_p = (DEPTH, BATCH, SEQ, N_KV_HEADS, HEAD_DIM)
    kv_s = (DEPTH, DEC_BATCH, 1, N_KV_HEADS, HEAD_DIM)
    st_p = (DEPTH, BATCH, SSM_GROUPS, SSM_STATE)
    st_s = (DEPTH, DEC_BATCH, SSM_GROUPS, SSM_STATE)
    return (yp.reshape(BATCH, SEQ, D_MODEL), ys.reshape(DEC_BATCH, 1, D_MODEL),
            kp.reshape(kv_p), vp.reshape(kv_p), cp, rp.reshape(st_p), ip.reshape(st_p),
            ksm.reshape(kv_s), vsm.reshape(kv_s), cs, rs.reshape(st_s), is_.reshape(st_s))
```

```python
import functools

import jax
import jax.numpy as jnp
from jax import lax
from jax.experimental import pallas as pl
from jax.experimental.pallas import tpu as pltpu

F32 = jnp.float32
BF16 = jnp.bfloat16

D_MODEL = 1024
DEPTH = 4
SEQ = 2048
BATCH = 8
DEC_BATCH = 128
PAGE_SIZE = 128
EPS = 1e-6
CONV_WIDTH = 3
SSM_GROUP = 16
SSM_GROUPS = D_MODEL // SSM_GROUP
SSM_STATE = 64
SSM_LANES = SSM_GROUPS * SSM_STATE
HEAD_DIM = 128
N_HEADS = D_MODEL // HEAD_DIM
N_KV_HEADS = N_HEADS // 2
KV_GROUP = N_HEADS // N_KV_HEADS
KV_DIM = N_KV_HEADS * HEAD_DIM
KV_GROUP_SHIFT = KV_GROUP.bit_length() - 1
assert KV_GROUP == 1 << KV_GROUP_SHIFT and N_KV_HEADS & (N_KV_HEADS - 1) == 0
PAGE_ROWS = PAGE_SIZE * N_KV_HEADS
MOBA_BLOCK = 256
MOBA_TOPK = 3
ATTN_SCALE = HEAD_DIM ** -0.5
NEG = -1e30
D_FF = 2816
D_IN = 9216
COL_CB, COL_CC, COL_CX, COL_U, COL_GA, COL_GB, COL_GC = 0, 1, 2, 3, 4, 5, 6
D_MIX = 7 * D_MODEL
COL_Q = 0
COL_K = D_MODEL
COL_V = COL_K + KV_DIM
D_QKV = D_MODEL + 2 * KV_DIM
W_QKV = 4 * D_MODEL

LANES = 128
SUBLANES = 8
MXU_DIM = 256
VMEM_LIMIT = 56 * 1024 * 1024

S5_SLABS = D_MODEL // MXU_DIM
S5_SLAB_LANES = SSM_LANES // S5_SLABS


def _params(*sem):
    return pltpu.CompilerParams(dimension_semantics=sem, vmem_limit_bytes=VMEM_LIMIT)


def _ada_kernel(c_ref, w_ref, b_ref, o_ref):
    c = jax.nn.silu(c_ref[...]).astype(BF16)
    o_ref[...] = jnp.dot(c, w_ref[...].astype(BF16), preferred_element_type=F32) + b_ref[...]


def _ada(c_all, w_ada, b_ada):
    n_rows = c_all.shape[0]
    tn = 1536
    return pl.pallas_call(
        _ada_kernel,
        out_shape=jax.ShapeDtypeStruct((DEPTH, n_rows, 6 * D_MODEL), F32),
        grid=(DEPTH, 6 * D_MODEL // tn),
        in_specs=[
            pl.BlockSpec((n_rows, D_MODEL), lambda l, j: (0, 0)),
            pl.BlockSpec((None, D_MODEL, tn), lambda l, j: (l, 0, j)),
            pl.BlockSpec((None, 1, tn), lambda l, j: (l, 0, j)),
        ],
        out_specs=pl.BlockSpec((None, n_rows, tn), lambda l, j: (l, 0, j)),
        compiler_params=_params("parallel", "parallel"),
    )(c_all, w_ada, b_ada.reshape(DEPTH, 1, 6 * D_MODEL))


def _s5_disc_kernel(lre_ref, lim_ref, ls_ref, bre_ref, bim_ref, lbre_ref, lbim_ref, bbre_ref, bbim_ref):
    lre = lre_ref[...]
    lim = lim_ref[...]
    step = jnp.exp(ls_ref[...])
    mag = jnp.exp(lre * step)
    ang = lim * step
    lbre = mag * jnp.cos(ang)
    lbim = mag * jnp.sin(ang)
    lbre_ref[...] = lbre
    lbim_ref[...] = lbim
    nre = lbre - 1.0
    den = lre * lre + lim * lim
    cre = (nre * lre + lbim * lim) / den
    cim = (lbim * lre - nre * lim) / den
    bre = bre_ref[...]
    bim = bim_ref[...]
    bbre_ref[...] = cre * bre - cim * bim
    bbim_ref[...] = cre * bim + cim * bre


def _s5_discretize(lam_re, lam_im, log_step, b_re_t, b_im_t):
    g, p, i = SSM_GROUPS, SSM_STATE, SSM_GROUP
    gp = pl.BlockSpec((None, g, 1, p), lambda l: (l, 0, 0, 0))
    gip = pl.BlockSpec((None, g, i, p), lambda l: (l, 0, 0, 0))
    return pl.pallas_call(
        _s5_disc_kernel,
        out_shape=(jax.ShapeDtypeStruct((DEPTH, g, 1, p), F32), jax.ShapeDtypeStruct((DEPTH, g, 1, p), F32),
                   jax.ShapeDtypeStruct((DEPTH, g, i, p), F32), jax.ShapeDtypeStruct((DEPTH, g, i, p), F32)),
        grid=(DEPTH,),
        in_specs=[gp, gp, pl.BlockSpec((None, g, 1, 1), lambda l: (l, 0, 0, 0)), gip, gip],
        out_specs=(gp, gp, gip, gip),
        compiler_params=_params("parallel"),
    )(lam_re.reshape(DEPTH, g, 1, p), lam_im.reshape(DEPTH, g, 1, p), log_step.reshape(DEPTH, g, 1, 1),
      b_re_t, b_im_t)


def _block_diag_in(b_t):
    return _tile_diag(b_t.reshape(DEPTH, S5_SLABS, MXU_DIM, SSM_STATE), SSM_GROUP, SSM_STATE)


def _block_diag_out(c):
    ct = jnp.swapaxes(c, 2, 3).reshape(DEPTH, S5_SLABS, S5_SLAB_LANES, SSM_GROUP)
    return _tile_diag(ct, SSM_STATE, SSM_GROUP)


def _tile_diag(a, rows_per_group, cols_per_group):
    gs = SSM_GROUPS // S5_SLABS
    tiled = jnp.tile(a, (1, 1, 1, gs))
    row_g = jnp.arange(gs * rows_per_group)[:, None] // rows_per_group
    col_g = jnp.arange(gs * cols_per_group)[None, :] // cols_per_group
    return jnp.where(row_g == col_g, tiled, 0.0)


def _mod_norm(x, g, scale, shift):
    y = x * lax.rsqrt(jnp.mean(x * x, axis=-1, keepdims=True) + EPS)
    return (y * g) * (1.0 + scale) + shift


IN_PROJ_CHUNK = 1024


def _in_proj_kernel(x_ref, g_ref, sc_ref, sh_ref, w_ref, mix_ref, qkv_ref):
    h = _mod_norm(x_ref[...], g_ref[...], sc_ref[...], sh_ref[...]).astype(BF16)
    nc = IN_PROJ_CHUNK
    for n in range(D_MIX // nc):
        src = n * nc if n * nc < W_QKV else n * nc + D_QKV
        mix_ref[:, n * nc:(n + 1) * nc] = jnp.dot(
            h, w_ref[:, src:src + nc], preferred_element_type=F32).astype(BF16)
    for n in range(D_QKV // nc):
        src = W_QKV + n * nc
        qkv_ref[:, n * nc:(n + 1) * nc] = jnp.dot(h, w_ref[:, src:src + nc], preferred_element_type=F32)


def _mod_spec(mod, mod5, layer, chunk, tm, rows_per_seq, decode):
    if decode:
        return mod, pl.BlockSpec((None, tm, D_MODEL), lambda i, *_: (layer, 0, chunk))
    return mod5, pl.BlockSpec((None, None, None, 1, D_MODEL),
                              lambda i, *_: (layer, (i * tm) // rows_per_seq, chunk, 0, 0))


def _in_proj(x, mods, layer, g_mix, w_in, decode):
    m = x.shape[0]
    tm = m if decode else 512
    sc_arr, sc_spec = _mod_spec(*mods, layer, 1, tm, SEQ, decode)
    sh_arr, sh_spec = _mod_spec(*mods, layer, 0, tm, SEQ, decode)
    return pl.pallas_call(
        _in_proj_kernel,
        out_shape=(jax.ShapeDtypeStruct((m, D_MIX), BF16), jax.ShapeDtypeStruct((m, D_QKV), F32)),
        grid=(m // tm,),
        in_specs=[
            pl.BlockSpec((tm, D_MODEL), lambda i: (i, 0)),
            pl.BlockSpec((None, 1, D_MODEL), lambda i: (layer, 0, 0)),
            sc_spec, sh_spec,
            pl.BlockSpec((None, D_MODEL, D_IN), lambda i: (layer, 0, 0), pipeline_mode=pl.Buffered(1)),
        ],
        out_specs=(pl.BlockSpec((tm, D_MIX), lambda i: (i, 0)), pl.BlockSpec((tm, D_QKV), lambda i: (i, 0))),
        compiler_params=_params("parallel"),
    )(x, g_mix, sc_arr, sh_arr, w_in)


def _s5_readout(u_slab, sre_slab, sim_slab, cre_ref, cim_ref, d_ref, s):
    y = (jnp.dot(sre_slab.astype(BF16), cre_ref[s], preferred_element_type=F32)
         - jnp.dot(sim_slab.astype(BF16), cim_ref[s], preferred_element_type=F32))
    y = y + u_slab * d_ref[:, s * MXU_DIM:(s + 1) * MXU_DIM]
    return jax.nn.gelu(y, approximate=True)


def _glu(g, wa_ref, wb_ref):
    gb = g.astype(BF16)
    return (jnp.dot(gb, wa_ref[...], preferred_element_type=F32)
            * jax.nn.sigmoid(jnp.dot(gb, wb_ref[...], preferred_element_type=F32)))


SCAN_LANES = 512


def _s5_prompt_kernel(u_ref, lre_ref, lim_ref, bre_ref, bim_ref, cre_ref, cim_ref, d_ref, wa_ref, wb_ref,
                      ob_ref, fre_ref, fim_ref,
                      us_ref, sre_ref, sim_ref, os_ref, st_ref, *, tt):
    n_lane_slabs = D_MODEL // LANES

    @pl.when(pl.program_id(0) == 0)
    def _():
        st_ref[...] = jnp.zeros_like(st_ref)

    for b in range(BATCH):
        for sl in range(n_lane_slabs):
            us_ref[sl, pl.ds(b, tt, stride=BATCH), :] = u_ref[b, :, sl * LANES:(sl + 1) * LANES].astype(F32)

    def u_slab(s):
        per = MXU_DIM // LANES
        return jnp.concatenate([us_ref[per * s + k] for k in range(per)], axis=1)

    gs = []
    for s in range(S5_SLABS):
        ub = u_slab(s).astype(BF16)
        slab = slice(s * S5_SLAB_LANES, (s + 1) * S5_SLAB_LANES)
        sre_ref[:, slab] = jnp.dot(ub, bre_ref[s], preferred_element_type=F32)
        sim_ref[:, slab] = jnp.dot(ub, bim_ref[s], preferred_element_type=F32)
        for c in range(S5_SLAB_LANES // SCAN_LANES):
            cols = slice(s * S5_SLAB_LANES + c * SCAN_LANES, s * S5_SLAB_LANES + (c + 1) * SCAN_LANES)
            lre = jnp.broadcast_to(lre_ref[:, cols], (BATCH, SCAN_LANES))
            lim = jnp.broadcast_to(lim_ref[:, cols], (BATCH, SCAN_LANES))
            pre, pim = st_ref[0, :, cols], st_ref[1, :, cols]
            for t in range(tt):
                rows = slice(t * BATCH, (t + 1) * BATCH)
                pre, pim = (lre * pre - lim * pim + sre_ref[rows, cols],
                            lre * pim + lim * pre + sim_ref[rows, cols])
                sre_ref[rows, cols] = pre
                sim_ref[rows, cols] = pim
            st_ref[0, :, cols] = pre
            st_ref[1, :, cols] = pim
        gs.append(_s5_readout(u_slab(s), sre_ref[:, slab], sim_ref[:, slab], cre_ref, cim_ref, d_ref, s))

    fre_ref[...] = st_ref[0]
    fim_ref[...] = st_ref[1]

    ob = _glu(jnp.concatenate(gs, axis=1), wa_ref, wb_ref)

    for sl in range(n_lane_slabs):
        os_ref[sl] = ob[:, sl * LANES:(sl + 1) * LANES]
    for b in range(BATCH):
        for sl in range(n_lane_slabs):
            ob_ref[b, :, sl * LANES:(sl + 1) * LANES] = os_ref[sl, pl.ds(b, tt, stride=BATCH), :].astype(BF16)


def _s5_decode_kernel(u_ref, s0re_ref, s0im_ref, lre_ref, lim_ref, bre_ref, bim_ref, cre_ref, cim_ref, d_ref,
                      wa_ref, wb_ref, ob_ref, fre_ref, fim_ref):
    gs = []
    for s in range(S5_SLABS):
        cols = slice(s * S5_SLAB_LANES, (s + 1) * S5_SLAB_LANES)
        ub = u_ref[:, s * MXU_DIM:(s + 1) * MXU_DIM]
        u = ub.astype(F32)
        lre, lim = lre_ref[:, cols], lim_ref[:, cols]
        pre, pim = s0re_ref[:, cols], s0im_ref[:, cols]
        nre = lre * pre - lim * pim + jnp.dot(ub, bre_ref[s], preferred_element_type=F32)
        nim = lre * pim + lim * pre + jnp.dot(ub, bim_ref[s], preferred_element_type=F32)
        fre_ref[:, cols] = nre
        fim_ref[:, cols] = nim
        gs.append(_s5_readout(u, nre, nim, cre_ref, cim_ref, d_ref, s))
    ob_ref[...] = _glu(jnp.concatenate(gs, axis=1), wa_ref, wb_ref)


def _s5_weight_specs(layer, n_grid):
    def const(shape, *idx):
        return pl.BlockSpec(shape, lambda *_: idx, pipeline_mode=pl.Buffered(1))

    return [
        const((None, 1, SSM_LANES), layer, 0, 0),
        const((None, 1, SSM_LANES), layer, 0, 0),
        const((None, S5_SLABS, MXU_DIM, S5_SLAB_LANES), layer, 0, 0, 0),
        const((None, S5_SLABS, MXU_DIM, S5_SLAB_LANES), layer, 0, 0, 0),
        const((None, S5_SLABS, S5_SLAB_LANES, MXU_DIM), layer, 0, 0, 0),
        const((None, S5_SLABS, S5_SLAB_LANES, MXU_DIM), layer, 0, 0, 0),
        const((None, 1, D_MODEL), layer, 0, 0),
        const((None, D_MODEL, D_MODEL), layer, 0, 0),
        const((None, D_MODEL, D_MODEL), layer, 0, 0),
    ]


def _s5_prompt(mix3, layer, s5w):
    tt = 64
    rows = tt * BATCH
    kern = functools.partial(_s5_prompt_kernel, tt=tt)
    state = jax.ShapeDtypeStruct((BATCH, SSM_LANES), F32)
    state_spec = pl.BlockSpec((BATCH, SSM_LANES), lambda i: (0, 0))
    return pl.pallas_call(
        kern,
        out_shape=(jax.ShapeDtypeStruct((BATCH, SEQ, D_MODEL), BF16), state, state),
        grid=(SEQ // tt,),
        in_specs=[pl.BlockSpec((BATCH, tt, D_MODEL), lambda i: (0, i, COL_U))] + _s5_weight_specs(layer, 1),
        out_specs=(pl.BlockSpec((BATCH, tt, D_MODEL), lambda i: (0, i, 0)), state_spec, state_spec),
        scratch_shapes=[
            pltpu.VMEM((D_MODEL // LANES, rows, LANES), F32),
            pltpu.VMEM((rows, SSM_LANES), F32),
            pltpu.VMEM((rows, SSM_LANES), F32),
            pltpu.VMEM((D_MODEL // LANES, rows, LANES), F32),
            pltpu.VMEM((2, BATCH, SSM_LANES), F32),
        ],
        compiler_params=_params("arbitrary"),
    )(mix3, *s5w)


def _s5_decode(mix, s0re, s0im, layer, s5w):
    m = mix.shape[0]
    state = jax.ShapeDtypeStruct((m, SSM_LANES), F32)
    full = pl.BlockSpec((m, SSM_LANES), lambda i: (0, 0))
    return pl.pallas_call(
        _s5_decode_kernel,
        out_shape=(jax.ShapeDtypeStruct((m, D_MODEL), F32), state, state),
        grid=(1,),
        in_specs=[pl.BlockSpec((m, D_MODEL), lambda i: (0, COL_U)),
                  pl.BlockSpec((None, m, SSM_LANES), lambda i: (layer, 0, 0)),
                  pl.BlockSpec((None, m, SSM_LANES), lambda i: (layer, 0, 0))] + _s5_weight_specs(layer, 1),
        out_specs=(pl.BlockSpec((m, D_MODEL), lambda i: (0, 0)), full, full),
        compiler_params=_params("arbitrary"),
    )(mix, s0re, s0im, *s5w)


def _topk_select(gate, valid, n_candidates):
    lane = lax.broadcasted_iota(jnp.int32, gate.shape, 1)
    g = jnp.where(valid, gate, NEG)
    rank = jnp.zeros(gate.shape, jnp.int32)
    for jp in range(n_candidates):
        col = g[:, jp:jp + 1]
        rank = rank + jnp.where(col > g, 1, jnp.where(col == g, jnp.where(lane > jp, 1, 0), 0))
    return jnp.where(valid, jnp.where(rank < MOBA_TOPK, 1.0, 0.0), 0.0)


def _moba_prompt_kernel(q_ref, k_ref, v_ref, *refs):
    o_ref, ko_ref, vo_ref = refs[-3:]
    nb = SEQ // MOBA_BLOCK
    k = k_ref[...]
    v = v_ref[...]
    head_rows = pl.ds(pl.program_id(1), SEQ, stride=N_KV_HEADS)
    ko_ref[head_rows, :] = k
    vo_ref[head_rows, :] = v
    assert nb <= SUBLANES
    vb = jnp.concatenate([v.astype(BF16), jnp.ones((SEQ, HEAD_DIM), BF16)], axis=1)
    key_blk = lax.broadcasted_iota(jnp.int32, (SEQ, LANES), 0) // MOBA_BLOCK
    onehot = jnp.where(key_blk == lax.broadcasted_iota(jnp.int32, (SEQ, LANES), 1), 1.0, 0.0)
    kb = jnp.concatenate([k.astype(BF16), onehot.astype(BF16)], axis=1)
    kmean = jnp.mean(k.reshape(nb, MOBA_BLOCK, HEAD_DIM), axis=1).astype(BF16)
    rows = KV_GROUP * MOBA_BLOCK
    blk = lax.broadcasted_iota(jnp.int32, (nb, rows), 0)
    q_in_blk = lax.broadcasted_iota(jnp.int32, (rows, MOBA_BLOCK), 0) & (MOBA_BLOCK - 1)
    k_in_blk = lax.broadcasted_iota(jnp.int32, (rows, MOBA_BLOCK), 1)
    causal = k_in_blk <= q_in_blk
    nt = (((1,), (1,)), ((), ()))
    for i in reversed(range(nb)):
        q2 = q_ref[i * MOBA_BLOCK:(i + 1) * MOBA_BLOCK, :]
        q = jnp.concatenate([q2[:, h * HEAD_DIM:(h + 1) * HEAD_DIM] for h in range(KV_GROUP)], axis=0)
        nk = (i + 1) * MOBA_BLOCK
        if i > 0:
            gate = lax.dot_general(kmean, q.astype(BF16), nt, preferred_element_type=F32)
            g = jnp.where(blk < i, gate, NEG)
            rank = jnp.zeros((nb, rows), jnp.int32)
            for jp in range(i):
                other = g[jp:jp + 1, :]
                rank = rank + jnp.where(other > g, 1, jnp.where(other == g, jnp.where(blk > jp, 1, 0), 0))
            bias = jnp.where(blk < i, jnp.where(rank < MOBA_TOPK, 0.0, NEG), 0.0)
            bias = jnp.concatenate([bias, jnp.zeros((LANES - nb, rows), F32)], axis=0).T
        else:
            bias = jnp.zeros((rows, LANES), F32)
        qa = jnp.concatenate([(q * ATTN_SCALE).astype(BF16), bias.astype(BF16)], axis=1)
        s = lax.dot_general(qa, kb[:nk], nt, preferred_element_type=F32)
        diag = jnp.where(causal, s[:, i * MOBA_BLOCK:], NEG)
        s = jnp.concatenate([s[:, :i * MOBA_BLOCK], diag], axis=1) if i > 0 else diag
        m = jnp.max(s, axis=-1, keepdims=True)
        p = jnp.exp(s - m)
        o = jnp.dot(p.astype(BF16), vb[:nk], preferred_element_type=F32)
        o = o[:, :HEAD_DIM] / o[:, HEAD_DIM:]
        for h in range(KV_GROUP):
            o_ref[i * MOBA_BLOCK:(i + 1) * MOBA_BLOCK, h * HEAD_DIM:(h + 1) * HEAD_DIM] = (
                o[h * MOBA_BLOCK:(h + 1) * MOBA_BLOCK].astype(BF16))


def _moba_prompt(proj3, layer, kv_stacks):
    gw = KV_GROUP * HEAD_DIM
    kv = jax.ShapeDtypeStruct((DEPTH, BATCH, SEQ * N_KV_HEADS, HEAD_DIM), F32)
    kv_spec = pl.BlockSpec((None, None, SEQ * N_KV_HEADS, HEAD_DIM), lambda b, h: (layer, b, 0, 0))
    in_specs = [
        pl.BlockSpec((None, SEQ, gw), lambda b, h: (b, 0, COL_Q // gw + h)),
        pl.BlockSpec((None, SEQ, HEAD_DIM), lambda b, h: (b, 0, COL_K // HEAD_DIM + h)),
        pl.BlockSpec((None, SEQ, HEAD_DIM), lambda b, h: (b, 0, COL_V // HEAD_DIM + h)),
    ]
    args = [proj3, proj3, proj3]
    aliases = {}
    if kv_stacks is not None:
        in_specs += [pl.BlockSpec(memory_space=pl.ANY)] * 2
        args += list(kv_stacks)
        aliases = {3: 1, 4: 2}
    out_c, k_stack, v_stack = pl.pallas_call(
        _moba_prompt_kernel,
        out_shape=(jax.ShapeDtypeStruct((BATCH, SEQ, D_MODEL), BF16), kv, kv),
        grid=(BATCH, N_KV_HEADS),
        in_specs=in_specs,
        out_specs=(pl.BlockSpec((None, SEQ, gw), lambda b, h: (b, 0, h)), kv_spec, kv_spec),
        input_output_aliases=aliases,
        compiler_params=_params("parallel", "arbitrary"),
    )(*args)
    return out_c, (k_stack, v_stack)


def _moba_decode_kernel(pt_ref, q_ref, kn_ref, vn_ref, *refs, n_pages):
    del pt_ref
    assert SUBLANES == 2 * N_KV_HEADS
    k_pages = refs[:n_pages]
    v_pages = refs[n_pages:2 * n_pages]
    o_ref = refs[2 * n_pages]
    per = MOBA_BLOCK // PAGE_SIZE
    nb = n_pages // per
    nt = (((1,), (1,)), ((), ()))
    q = q_ref[...]
    qs = (q * ATTN_SCALE).astype(BF16)
    head_kv = lax.broadcasted_iota(jnp.int32, (N_HEADS, PAGE_ROWS), 0) >> KV_GROUP_SHIFT
    row_kv = lax.broadcasted_iota(jnp.int32, (N_HEADS, PAGE_ROWS), 1) & (N_KV_HEADS - 1)
    own_kv = row_kv == head_kv
    scores, ksums = [], []
    for pg in range(n_pages):
        kp = k_pages[pg][...]
        scores.append(lax.dot_general(qs, kp.astype(BF16), nt, preferred_element_type=F32))
        ksums.append(jnp.sum(kp.reshape(PAGE_ROWS // SUBLANES, SUBLANES, HEAD_DIM), axis=0))
    kmeans = []
    for j in range(nb):
        ks = ksums[per * j]
        for r in range(1, per):
            ks = ks + ksums[per * j + r]
        ks = ks + pltpu.roll(ks, N_KV_HEADS, 0)
        kmeans.append(ks * (1.0 / MOBA_BLOCK))
    kmean = jnp.concatenate(kmeans, axis=0).astype(BF16)
    gate = lax.dot_general(q.astype(BF16), kmean, nt, preferred_element_type=F32)
    n_cand = nb * SUBLANES
    cand = lax.broadcasted_iota(jnp.int32, (N_HEADS, n_cand), 1) & (SUBLANES - 1)
    cand_head_kv = lax.broadcasted_iota(jnp.int32, (N_HEADS, n_cand), 0) >> KV_GROUP_SHIFT
    sel = _topk_select(gate, cand == cand_head_kv, n_cand)
    masked = []
    for j in range(nb):
        on = jnp.max(sel[:, j * SUBLANES:(j + 1) * SUBLANES], axis=-1, keepdims=True) > 0.0
        for r in range(per):
            masked.append(jnp.where(own_kv, jnp.where(on, scores[per * j + r], NEG), NEG))
    kn = kn_ref[...]
    vn = vn_ref[...]
    kn_h = jnp.concatenate([kn[h // KV_GROUP:h // KV_GROUP + 1] for h in range(N_HEADS)], axis=0)
    vn_h = jnp.concatenate([vn[h // KV_GROUP:h // KV_GROUP + 1] for h in range(N_HEADS)], axis=0)
    s_own = jnp.sum(q * ATTN_SCALE * kn_h, axis=-1, keepdims=True)
    m = jnp.maximum(jnp.max(functools.reduce(jnp.maximum, masked), axis=-1, keepdims=True), s_own)
    p_own = jnp.exp(s_own - m)
    acc = p_own * vn_h
    p_sum = jnp.zeros((N_HEADS, PAGE_ROWS), F32)
    for pg in range(n_pages):
        p = jnp.exp(masked[pg] - m)
        p_sum = p_sum + p
        acc = acc + jnp.dot(p.astype(BF16), v_pages[pg][...].astype(BF16), preferred_element_type=F32)
    l = jnp.sum(p_sum, axis=-1, keepdims=True) + p_own
    o_ref[...] = acc / l


def _moba_decode(proj, cache_k, cache_v, page_table, layer):
    m = proj.shape[0]
    n_pages = page_table.shape[1]
    q = proj[:, COL_Q:COL_Q + D_MODEL].reshape(m, N_HEADS, HEAD_DIM)
    kn = proj[:, COL_K:COL_K + KV_DIM].reshape(m, N_KV_HEADS, HEAD_DIM)
    vn = proj[:, COL_V:COL_V + KV_DIM].reshape(m, N_KV_HEADS, HEAD_DIM)

    def page_spec(pg):
        return pl.BlockSpec((None, None, PAGE_ROWS, HEAD_DIM), lambda b, pt: (layer, pt[b, pg], 0, 0))

    grid_spec = pltpu.PrefetchScalarGridSpec(
        num_scalar_prefetch=1,
        grid=(m,),
        in_specs=[
            pl.BlockSpec((None, N_HEADS, HEAD_DIM), lambda b, pt: (b, 0, 0)),
            pl.BlockSpec((None, N_KV_HEADS, HEAD_DIM), lambda b, pt: (b, 0, 0)),
            pl.BlockSpec((None, N_KV_HEADS, HEAD_DIM), lambda b, pt: (b, 0, 0)),
        ] + [page_spec(pg) for pg in range(n_pages)] + [page_spec(pg) for pg in range(n_pages)],
        out_specs=pl.BlockSpec((None, N_HEADS, HEAD_DIM), lambda b, pt: (b, 0, 0)),
    )
    out = pl.pallas_call(
        functools.partial(_moba_decode_kernel, n_pages=n_pages),
        out_shape=jax.ShapeDtypeStruct((m, N_HEADS, HEAD_DIM), F32),
        grid_spec=grid_spec,
        compiler_params=_params("parallel"),
    )(page_table, q, kn, vn, *([cache_k] * n_pages), *([cache_v] * n_pages))
    return out.reshape(m, D_MODEL)


def _sigmoid(x):
    return 0.5 * jnp.tanh(0.5 * x) + 0.5


def _merge_tail(x, gate, cb, zc, ga, gb, gc, ob, oc, wc_ref, wo_ref):
    out_a = jnp.dot((cb.astype(F32) * zc).astype(BF16), wc_ref[...], preferred_element_type=F32)
    merged = (_sigmoid(ga.astype(F32)) * out_a + _sigmoid(gb.astype(F32)) * ob.astype(F32)
              + _sigmoid(gc.astype(F32)) * oc.astype(F32))
    return x + gate * jnp.dot(merged.astype(BF16), wo_ref[...], preferred_element_type=F32)


def _mix_prompt_kernel(x_ref, gate_ref, cb_ref, cc_ref, cx_ref, ga_ref, gb_ref, gc_ref, ob_ref, oc_ref,
                       cw_ref, wc_ref, wo_ref, xo_ref, last_ref, z_ref, *, tm, tiles_per_seq):
    tail = SUBLANES

    @pl.when(pl.program_id(0) % tiles_per_seq == 0)
    def _():
        z_ref[0:tail, :] = jnp.zeros((tail, D_MODEL), F32)

    z = cc_ref[...].astype(F32) * cx_ref[...].astype(F32)
    z_ref[tail:tail + tm, :] = z
    w = cw_ref[...]
    zc = (z_ref[tail - 2:tail - 2 + tm, :] * w[0:1] + z_ref[tail - 1:tail - 1 + tm, :] * w[1:2] + z * w[2:3])
    z_last = z[tm - tail:, :]
    z_ref[0:tail, :] = z_last
    last_ref[...] = z_last
    xo_ref[...] = _merge_tail(x_ref[...], gate_ref[...], cb_ref[...], zc, ga_ref[...], gb_ref[...], gc_ref[...],
                              ob_ref[...], oc_ref[...], wc_ref, wo_ref)


def _mix_decode_kernel(x_ref, gate_ref, cb_ref, cc_ref, cx_ref, ga_ref, gb_ref, gc_ref, ob_ref, oc_ref,
                       cw_ref, wc_ref, wo_ref, p0_ref, p1_ref, xo_ref, last_ref):
    z = cc_ref[...].astype(F32) * cx_ref[...].astype(F32)
    w = cw_ref[...]
    p1 = p1_ref[...]
    zc = p0_ref[...] * w[0:1] + p1 * w[1:2] + z * w[2:3]
    last_ref[:, 0:D_MODEL] = p1
    last_ref[:, D_MODEL:2 * D_MODEL] = z
    xo_ref[...] = _merge_tail(x_ref[...], gate_ref[...], cb_ref[...], zc, ga_ref[...], gb_ref[...], gc_ref[...],
                              ob_ref[...], oc_ref[...], wc_ref, wo_ref)


def _mix(x, mods, layer, proj, out_b, out_c, conv_w, w_conv_out, w_o, decode, conv_state=None):
    m = x.shape[0]
    tm = m if decode else 512
    gate_arr, gate_spec = _mod_spec(*mods, layer, 2, tm, SEQ, decode)

    def col(c):
        return pl.BlockSpec((tm, D_MODEL), lambda i: (i, c))

    row = pl.BlockSpec((tm, D_MODEL), lambda i: (i, 0))
    wspec = pl.BlockSpec((None, D_MODEL, D_MODEL), lambda i: (layer, 0, 0))
    in_specs = [row, gate_spec, col(COL_CB), col(COL_CC), col(COL_CX), col(COL_GA), col(COL_GB), col(COL_GC),
                row, row, pl.BlockSpec((None, CONV_WIDTH, D_MODEL), lambda i: (layer, 0, 0)), wspec, wspec]
    args = [x, gate_arr, proj, proj, proj, proj, proj, proj, out_b, out_c, conv_w, w_conv_out, w_o]
    if decode:
        state2 = conv_state.reshape(DEPTH, m, (CONV_WIDTH - 1) * D_MODEL)
        in_specs += [pl.BlockSpec((None, m, D_MODEL), lambda i: (layer, 0, 0)),
                     pl.BlockSpec((None, m, D_MODEL), lambda i: (layer, 0, 1))]
        args += [state2, state2]
        return pl.pallas_call(
            _mix_decode_kernel,
            out_shape=(jax.ShapeDtypeStruct((m, D_MODEL), F32),
                       jax.ShapeDtypeStruct((m, (CONV_WIDTH - 1) * D_MODEL), F32)),
            grid=(1,),
            in_specs=in_specs,
            out_specs=(row, pl.BlockSpec((m, (CONV_WIDTH - 1) * D_MODEL), lambda i: (0, 0))),
            compiler_params=_params("arbitrary"),
        )(*args)
    tiles_per_seq = SEQ // tm
    return pl.pallas_call(
        functools.partial(_mix_prompt_kernel, tm=tm, tiles_per_seq=tiles_per_seq),
        out_shape=(jax.ShapeDtypeStruct((m, D_MODEL), F32),
                   jax.ShapeDtypeStruct((BATCH, SUBLANES, D_MODEL), F32)),
        grid=(m // tm,),
        in_specs=in_specs,
        out_specs=(row, pl.BlockSpec((None, SUBLANES, D_MODEL), lambda i: (i // tiles_per_seq, 0, 0))),
        scratch_shapes=[pltpu.VMEM((SUBLANES + tm, D_MODEL), F32)],
        compiler_params=_params("arbitrary"),
    )(*args)


FFN_CHUNKS = (MXU_DIM * (D_FF // MXU_DIM - D_FF // MXU_DIM // 2), MXU_DIM * (D_FF // MXU_DIM // 2))
assert sum(FFN_CHUNKS) == D_FF


def _ffn_kernel(x_ref, g_ref, sc_ref, sh_ref, gate_ref, w1_ref, w3_ref, w2_ref, gf_ref, o_ref, *, final_norm):
    x = x_ref[...]
    h = _mod_norm(x, g_ref[...], sc_ref[...], sh_ref[...]).astype(BF16)
    acc = None
    start = 0
    for width in FFN_CHUNKS:
        cols = slice(start, start + width)
        start += width
        a = (jax.nn.silu(jnp.dot(h, w1_ref[:, cols], preferred_element_type=F32))
             * jnp.dot(h, w3_ref[:, cols], preferred_element_type=F32))
        part = jnp.dot(a.astype(BF16), w2_ref[cols, :], preferred_element_type=F32)
        acc = part if acc is None else acc + part
    y = x + gate_ref[...] * acc
    if final_norm:
        y = y * lax.rsqrt(jnp.mean(y * y, axis=-1, keepdims=True) + EPS) * gf_ref[...]
    o_ref[...] = y


def _ffn(x, mods, layer, g_ffn, w1, w3, w2, g_final, decode):
    m = x.shape[0]
    tm = m if decode else 512
    sc_arr, sc_spec = _mod_spec(*mods, layer, 4, tm, SEQ, decode)
    sh_arr, sh_spec = _mod_spec(*mods, layer, 3, tm, SEQ, decode)
    gt_arr, gt_spec = _mod_spec(*mods, layer, 5, tm, SEQ, decode)
    row = pl.BlockSpec((tm, D_MODEL), lambda i: (i, 0))

    def resident(shape):
        return pl.BlockSpec((None,) + shape, lambda i: (layer, 0, 0), pipeline_mode=pl.Buffered(1))

    return pl.pallas_call(
        functools.partial(_ffn_kernel, final_norm=layer == DEPTH - 1),
        out_shape=jax.ShapeDtypeStruct((m, D_MODEL), F32),
        grid=(m // tm,),
        in_specs=[row, pl.BlockSpec((None, 1, D_MODEL), lambda i: (layer, 0, 0)), sc_spec, sh_spec, gt_spec,
                  resident((D_MODEL, D_FF)), resident((D_MODEL, D_FF)), resident((D_FF, D_MODEL)),
                  pl.BlockSpec((1, D_MODEL), lambda i: (0, 0))],
        out_specs=row,
        compiler_params=_params("parallel"),
    )(x, g_ffn, sc_arr, sh_arr, gt_arr, w1, w3, w2, g_final.reshape(1, D_MODEL))


def _trunk(x, mods, w, decode, state=None):
    ks, vs, convs, res, ims = [], [], [], [], []
    kv_stacks = None
    for l in range(DEPTH):
        mix, qkv = _in_proj(x, mods, l, w["g_mix"], w["w_in"], decode)
        if decode:
            out_b, s_re, s_im = _s5_decode(mix, state["ssm_re"], state["ssm_im"], l, w["s5"])
            out_c = _moba_decode(qkv, state["cache_k"], state["cache_v"], state["page_table"], l)
            ks.append(qkv[:, COL_K:COL_K + KV_DIM])
            vs.append(qkv[:, COL_V:COL_V + KV_DIM])
            x, conv_last = _mix(x, mods, l, mix, out_b, out_c, w["conv_w"], w["w_conv_out"], w["w_o"], True,
                                state["conv"])
            conv_last = conv_last.reshape(DEC_BATCH, CONV_WIDTH - 1, D_MODEL)
        else:
            out_b, s_re, s_im = _s5_prompt(mix.reshape(BATCH, SEQ, D_MIX), l, w["s5"])
            out_c, kv_stacks = _moba_prompt(qkv.reshape(BATCH, SEQ, D_QKV), l, kv_stacks)
            x, conv_last = _mix(x, mods, l, mix, out_b.reshape(BATCH * SEQ, D_MODEL),
                                out_c.reshape(BATCH * SEQ, D_MODEL), w["conv_w"], w["w_conv_out"], w["w_o"], False)
            conv_last = conv_last[:, SUBLANES - (CONV_WIDTH - 1):, :]
        x = _ffn(x, mods, l, w["g_ffn"], w["w_ff1"], w["w_ff3"], w["w_ff2"], w["g_final"], decode)
        convs.append(conv_last)
        res.append(s_re)
        ims.append(s_im)
    y = x
    k_all, v_all = (jnp.stack(ks), jnp.stack(vs)) if decode else kv_stacks
    return y, k_all, v_all, jnp.stack(convs), jnp.stack(res), jnp.stack(ims)


def kernel(x_prompt, x_sample, cache_k, cache_v, state_conv, state_ssm_re, state_ssm_im, page_table, c_prompt, c_sample, w_ada, b_ada, g_mix, w_in, conv_w, w_conv_out, ssm_lam_re, ssm_lam_im, ssm_log_step, ssm_b_re, ssm_b_im, ssm_c_re, ssm_c_im, ssm_d, w_glu_a, w_glu_b, w_o, g_ffn, w_ff1, w_ff3, w_ff2, g_final):
    n_pool = cache_k.shape[1]
    mod = _ada(jnp.concatenate([c_sample, c_prompt], axis=0), w_ada, b_ada)
    mods = (mod, mod[:, DEC_BATCH:].reshape(DEPTH, BATCH, 6, 1, D_MODEL))

    lam_re, lam_im, bb_re, bb_im = _s5_discretize(
        ssm_lam_re, ssm_lam_im, ssm_log_step, jnp.swapaxes(ssm_b_re, 2, 3), jnp.swapaxes(ssm_b_im, 2, 3))
    s5w = (lam_re.reshape(DEPTH, 1, SSM_LANES), lam_im.reshape(DEPTH, 1, SSM_LANES),
           _block_diag_in(bb_re).astype(BF16), _block_diag_in(bb_im).astype(BF16),
           _block_diag_out(ssm_c_re).astype(BF16), _block_diag_out(ssm_c_im).astype(BF16),
           ssm_d.reshape(DEPTH, 1, D_MODEL), w_glu_a.astype(BF16), w_glu_b.astype(BF16))
    w = dict(g_mix=g_mix.reshape(DEPTH, 1, D_MODEL), w_in=w_in.astype(BF16), s5=s5w, conv_w=conv_w,
             w_conv_out=w_conv_out.astype(BF16), w_o=w_o.astype(BF16), g_ffn=g_ffn.reshape(DEPTH, 1, D_MODEL),
             w_ff1=w_ff1.astype(BF16), w_ff3=w_ff3.astype(BF16), w_ff2=w_ff2.astype(BF16), g_final=g_final)

    yp, kp, vp, cp, rp, ip = _trunk(x_prompt.reshape(BATCH * SEQ, D_MODEL), mods, w, False)
    state = dict(ssm_re=state_ssm_re.reshape(DEPTH, DEC_BATCH, SSM_LANES),
                 ssm_im=state_ssm_im.reshape(DEPTH, DEC_BATCH, SSM_LANES),
                 cache_k=cache_k.reshape(DEPTH, n_pool, PAGE_ROWS, HEAD_DIM),
                 cache_v=cache_v.reshape(DEPTH, n_pool, PAGE_ROWS, HEAD_DIM),
                 page_table=page_table, conv=state_conv)
    ys, ksm, vsm, cs, rs, is_ = _trunk(x_sample.reshape(DEC_BATCH, D_MODEL), mods, w, True, state)

    kv_p = (DEPTH, BATCH, SEQ, N_KV_HEADS, HEAD_DIM)
    kv_s = (DEPTH, DEC_BATCH, 1, N_KV_HEADS, HEAD_DIM)
    st_p = (DEPTH, BATCH, SSM_GROUPS, SSM_STATE)
    st_s = (DEPTH, DEC_BATCH, SSM_GROUPS, SSM_STATE)
    return (yp.reshape(BATCH, SEQ, D_MODEL), ys.reshape(DEC_BATCH, 1, D_MODEL),
            kp.reshape(kv_p), vp.reshape(kv_p), cp, rp.reshape(st_p), ip.reshape(st_p),
            ksm.reshape(kv_s), vsm.reshape(kv_s), cs, rs.reshape(st_s), is_.reshape(st_s))
```
